```python
import math
import jax
import jax.numpy as jnp
from jax import lax
import numpy as np

D_MODEL = 1024
BATCH = 8
SEQ = 8192
DEPTH = 2

GRID_W = 64
CTX_LEN = 256
HEAD_DIM = 64
F32 = jnp.float32
RG_WIDTH = 256
RG_BLOCKS = 4
RG_BLOCK = RG_WIDTH // RG_BLOCKS
RG_CONV = 4
RG_C = 8.0
NA_HEADS = 6
NA_WIDTH = NA_HEADS * HEAD_DIM
NA_WIN_R = 8
NA_WIN_C = 16
GQA_HEADS = 6
GQA_KV_HEADS = 2
GQA_GROUP = GQA_HEADS // GQA_KV_HEADS
GQA_WIDTH = GQA_HEADS * HEAD_DIM
GQA_KV_WIDTH = GQA_KV_HEADS * HEAD_DIM
GQA_WINDOW = 128
GQA_BLOCK = 128
ROPE_BASE = 10000.0
KV_SIZES = (RG_WIDTH, NA_WIDTH, NA_WIDTH, GQA_KV_WIDTH, GQA_KV_WIDTH)
Q_SIZES = (RG_WIDTH, NA_WIDTH, GQA_WIDTH)
PROJ_KV = sum(KV_SIZES)
PROJ_WIDTH = PROJ_KV + sum(Q_SIZES)
MIX_WIDTH = RG_WIDTH + NA_WIDTH + GQA_WIDTH
N_GROUPS = 4
EXPERTS_PER_GROUP = 8
N_EXPERTS = N_GROUPS * EXPERTS_PER_GROUP
TOP_K = 2
EXPERT_FF = 512
MOE_BLOCK = 128
DN_ALPHA = (2 * DEPTH) ** 0.25
DN_BETA = (8 * DEPTH) ** -0.25
LN_EPS = 1e-5

kernel_name = 'hymba_style_rglru_na_swa_hmoe_diffusion_block'


def layer_norm(x, g, b):
    xf = x.astype(F32)
    mu = jnp.mean(xf, -1, keepdims=True)
    var = jnp.mean(jnp.square(xf - mu), -1, keepdims=True)
    return ((xf - mu) * lax.rsqrt(var + LN_EPS) * g + b).astype(x.dtype)


def modulate(x, shift, scale):
    return x * (1 + scale) + shift


def split_cols(p, sizes):
    idx = np.cumsum(sizes)[:-1].tolist()
    return jnp.split(p, idx, axis=-1)


def heads(t, n):
    return t.reshape(t.shape[:-1] + (n, HEAD_DIM))


def centred_dwconv(x, w, b):
    y = lax.conv_general_dilated(x, w[:, None, :].astype(x.dtype), window_strides=(1,),
                                 padding=[(RG_CONV // 2, RG_CONV - 1 - RG_CONV // 2)],
                                 dimension_numbers=('NWC', 'WIO', 'NWC'),
                                 feature_group_count=x.shape[-1])
    return y + b


def block_diag(x, w, b):
    xs = x.reshape(x.shape[:-1] + (RG_BLOCKS, RG_BLOCK))
    return jnp.einsum('blnd,nde->blne', xs, w).reshape(x.shape) + b


def rglru_coeffs(x, wa, ba, wi, bi, lam):
    r = jax.nn.sigmoid(block_diag(x, wa, ba).astype(F32))
    i = jax.nn.sigmoid(block_diag(x, wi, bi).astype(F32))
    log_a = -RG_C * r * jax.nn.softplus(-lam.astype(F32))
    a = jnp.exp(log_a)
    b = jnp.sqrt(-jnp.expm1(2.0 * log_a)) * (i * x.astype(F32))
    return a, b


def _combine(left, right):
    a_l, b_l = left
    a_r, b_r = right
    return a_l * a_r, a_r * b_l + b_r


def linear_scan(a, b, h0):
    a_cum, b_cum = lax.associative_scan(_combine, (a, b), axis=1)
    return a_cum * h0[:, None] + b_cum


def rglru_mixer(x_lat, x_ctx, gate_lat, gate_ctx, conv_w, conv_b, wa, ba, wi, bi, lam):
    xl = centred_dwconv(x_lat, conv_w, conv_b)
    xc = centred_dwconv(x_ctx, conv_w, conv_b)
    h0 = jnp.zeros((xl.shape[0], RG_WIDTH), F32)
    a_c, b_c = rglru_coeffs(xc, wa[0], ba[0], wi[0], bi[0], lam[0])
    a_l, b_l = rglru_coeffs(xl, wa[0], ba[0], wi[0], bi[0], lam[0])
    hc_f = linear_scan(a_c, b_c, h0)
    hl_f = linear_scan(a_l, b_l, hc_f[:, -1])
    a_c, b_c = rglru_coeffs(xc[:, ::-1], wa[1], ba[1], wi[1], bi[1], lam[1])
    a_l, b_l = rglru_coeffs(xl[:, ::-1], wa[1], ba[1], wi[1], bi[1], lam[1])
    hc_b = linear_scan(a_c, b_c, h0)
    hl_b = linear_scan(a_l, b_l, hc_b[:, -1])[:, ::-1]
    y_lat = (hl_f + hl_b).astype(x_lat.dtype) * jax.nn.gelu(gate_lat)
    if gate_ctx is None:
        return y_lat, None
    y_ctx = (hc_f + hc_b[:, ::-1]).astype(x_ctx.dtype) * jax.nn.gelu(gate_ctx)
    return y_lat, y_ctx


def axial_rope(x, row, col):
    half = x.shape[-1] // 2
    quarter = half // 2
    inv = ROPE_BASE ** (-jnp.arange(quarter, dtype=F32) / quarter)

    def rot(xp, pos):
        ang = pos.astype(F32)[:, None] * inv
        cos = jnp.cos(ang)[None, :, None]
        sin = jnp.sin(ang)[None, :, None]
        x1 = xp[..., :quarter].astype(F32)
        x2 = xp[..., quarter:].astype(F32)
        return jnp.concatenate([x1 * cos - x2 * sin, x2 * cos + x1 * sin], -1)

    return jnp.concatenate([rot(x[..., :half], row), rot(x[..., half:], col)], -1).astype(x.dtype)


def ctx_attention(q, k, v, sink):
    B, C, H, dh = q.shape
    hk = k.shape[2]
    g = H // hk
    qg = q.reshape(B, C, hk, g, dh)
    s = jnp.einsum('bqkgd,bjkd->bkgqj', qg, k).astype(F32) * dh ** -0.5
    if sink is None:
        p = jax.nn.softmax(s, -1)
    else:
        sink_col = jnp.broadcast_to(sink.astype(F32).reshape(1, hk, g, 1, 1), (B, hk, g, C, 1))
        p = jax.nn.softmax(jnp.concatenate([s, sink_col], -1), -1)[..., :-1]
    o = jnp.einsum('bkgqj,bjkd->bqkgd', p.astype(v.dtype), v)
    return o.reshape(B, C, H * dh)


def neighbourhood_attention(q, k, v, k_ctx, v_ctx, rpb):
    B, S, H, dh = q.shape
    rows = S // GRID_W
    wr = min(NA_WIN_R, rows)
    n_nb = wr * NA_WIN_C
    qg = q.reshape(B, rows, GRID_W, H, dh)
    kg = k.reshape(B, rows, GRID_W, H, dh)
    vg = v.reshape(B, rows, GRID_W, H, dh)
    cols = jnp.arange(GRID_W)
    col_idx = jnp.clip(cols - NA_WIN_C // 2, 0, GRID_W - NA_WIN_C)[:, None] + jnp.arange(NA_WIN_C)
    col_bias_idx = col_idx - cols[:, None] + NA_WIN_C - 1
    rpb_f = rpb.astype(F32)
    scale = dh ** -0.5

    def row_block(args):
        r, q_row = args
        rs = jnp.clip(r - NA_WIN_R // 2, 0, rows - wr)
        k_rows = lax.dynamic_slice_in_dim(kg, rs, wr, axis=1)
        v_rows = lax.dynamic_slice_in_dim(vg, rs, wr, axis=1)
        k_win = k_rows[:, :, col_idx]
        v_win = v_rows[:, :, col_idx]
        row_bias_idx = rs + jnp.arange(wr) - r + NA_WIN_R - 1
        bias = rpb_f[:, row_bias_idx][:, :, col_bias_idx].transpose(0, 2, 1, 3)
        s_nb = jnp.einsum('bqhd,brqchd->bhqrc', q_row, k_win).astype(F32) * scale + bias
        s_cx = jnp.einsum('bqhd,bkhd->bhqk', q_row, k_ctx).astype(F32) * scale
        s = jnp.concatenate([s_nb.reshape(B, H, GRID_W, n_nb), s_cx], -1)
        p = jax.nn.softmax(s, -1).astype(v.dtype)
        p_nb = p[..., :n_nb].reshape(B, H, GRID_W, wr, NA_WIN_C)
        return (jnp.einsum('bhqrc,brqchd->bqhd', p_nb, v_win)
                + jnp.einsum('bhqk,bkhd->bqhd', p[..., n_nb:], v_ctx))

    o = lax.map(row_block, (jnp.arange(rows), jnp.moveaxis(qg, 1, 0)))
    return jnp.moveaxis(o, 0, 1).reshape(B, S, H * dh)


def window_gqa(q, k, v, k_ctx, v_ctx, sink):
    B, S, H, dh = q.shape
    n_blk = S // GQA_BLOCK
    span = GQA_BLOCK + 2 * GQA_WINDOW
    C = k_ctx.shape[1]
    qb = q.reshape(B, n_blk, GQA_BLOCK, GQA_KV_HEADS, GQA_GROUP, dh)
    pad = ((0, 0), (GQA_WINDOW, GQA_WINDOW), (0, 0), (0, 0))
    k_pad = jnp.pad(k, pad)
    v_pad = jnp.pad(v, pad)
    sink_col = jnp.broadcast_to(sink.astype(F32).reshape(1, GQA_KV_HEADS, GQA_GROUP, 1, 1),
                                (B, GQA_KV_HEADS, GQA_GROUP, GQA_BLOCK, 1))
    scale = dh ** -0.5

    def block(args):
        bi, q_blk = args
        start = bi * GQA_BLOCK
        k_w = lax.dynamic_slice_in_dim(k_pad, start, span, axis=1)
        v_w = lax.dynamic_slice_in_dim(v_pad, start, span, axis=1)
        qpos = start + jnp.arange(GQA_BLOCK)
        kpos = start - GQA_WINDOW + jnp.arange(span)
        valid = (jnp.abs(qpos[:, None] - kpos[None, :]) <= GQA_WINDOW) & (kpos >= 0) & (kpos < S)
        s_loc = jnp.einsum('bqkgd,bjkd->bkgqj', q_blk, k_w).astype(F32) * scale
        s_loc = jnp.where(valid, s_loc, -jnp.inf)
        s_cx = jnp.einsum('bqkgd,bjkd->bkgqj', q_blk, k_ctx).astype(F32) * scale
        p = jax.nn.softmax(jnp.concatenate([s_loc, s_cx, sink_col], -1), -1).astype(v.dtype)
        return (jnp.einsum('bkgqj,bjkd->bqkgd', p[..., :span], v_w)
                + jnp.einsum('bkgqj,bjkd->bqkgd', p[..., span:span + C], v_ctx))

    o = lax.map(block, (jnp.arange(n_blk), jnp.moveaxis(qb, 1, 0)))
    return jnp.moveaxis(o, 0, 1).reshape(B, S, H * dh)


def mixer(h_lat, h_ctx, row, col, w_in, conv_w, conv_b, rg_a_w, rg_a_b, rg_i_w, rg_i_b, rg_lambda,
          na_rpb, gqa_sink, w_out, need_ctx):
    p_lat = h_lat @ w_in
    p_ctx = h_ctx @ (w_in if need_ctx else w_in[:, :PROJ_KV])
    rg_x, na_k, na_v, g_k, g_v, rg_gate, na_q, g_q = split_cols(p_lat, KV_SIZES + Q_SIZES)
    c_rg_x, c_na_k, c_na_v, c_g_k, c_g_v = split_cols(p_ctx[..., :PROJ_KV], KV_SIZES)
    c_rg_gate = None
    if need_ctx:
        c_rg_gate, c_na_q, c_g_q = split_cols(p_ctx[..., PROJ_KV:], Q_SIZES)
    y_rg, y_rg_c = rglru_mixer(rg_x, c_rg_x, rg_gate, c_rg_gate, conv_w, conv_b,
                               rg_a_w, rg_a_b, rg_i_w, rg_i_b, rg_lambda)
    k_na_c = heads(c_na_k, NA_HEADS)
    v_na_c = heads(c_na_v, NA_HEADS)
    k_ga_c = heads(c_g_k, GQA_KV_HEADS)
    v_ga_c = heads(c_g_v, GQA_KV_HEADS)
    o_na = neighbourhood_attention(heads(na_q, NA_HEADS), heads(na_k, NA_HEADS), heads(na_v, NA_HEADS),
                                   k_na_c, v_na_c, na_rpb)
    o_ga = window_gqa(axial_rope(heads(g_q, GQA_HEADS), row, col),
                      axial_rope(heads(g_k, GQA_KV_HEADS), row, col),
                      heads(g_v, GQA_KV_HEADS), k_ga_c, v_ga_c, gqa_sink)
    out_lat = jnp.concatenate([y_rg, o_na, o_ga], -1) @ w_out
    if not need_ctx:
        return out_lat, None
    o_na_c = ctx_attention(heads(c_na_q, NA_HEADS), k_na_c, v_na_c, None)
    o_ga_c = ctx_attention(heads(c_g_q, GQA_HEADS), k_ga_c, v_ga_c, gqa_sink)
    out_ctx = jnp.concatenate([y_rg_c, o_na_c, o_ga_c], -1) @ w_out
    return out_lat, out_ctx


def hier_route(h, wg, bg, we, be):
    pg = jax.nn.softmax((h @ wg).astype(F32) + bg.astype(F32), -1)
    p_top_g, g_idx = lax.top_k(pg, 1)
    le = ((h @ we).astype(F32) + be.astype(F32)).reshape(-1, N_GROUPS, EXPERTS_PER_GROUP)
    le_g = jnp.take_along_axis(le, g_idx[:, :, None], axis=1)[:, 0]
    top_l, e_local = lax.top_k(le_g, TOP_K)
    w = jax.nn.softmax(top_l, -1) * p_top_g
    return g_idx * EXPERTS_PER_GROUP + e_local, w


def hier_moe(h, wg, bg, we, be, w_gate, w_up, w_down):
    N, D = h.shape
    expert_id, gate = hier_route(h, wg, bg, we, be)
    M = N * TOP_K
    flat_e = expert_id.reshape(M)
    flat_tok = jnp.arange(M, dtype=jnp.int32) // TOP_K
    flat_w = gate.reshape(M)
    order = jnp.argsort(flat_e, stable=True)
    se, stok, sw = flat_e[order], flat_tok[order], flat_w[order]
    counts = jnp.bincount(flat_e, length=N_EXPERTS)
    padded = (counts + MOE_BLOCK - 1) // MOE_BLOCK * MOE_BLOCK
    end_pad = jnp.cumsum(padded)
    start_pad = end_pad - padded
    start_sorted = jnp.cumsum(counts) - counts
    dest = start_pad[se] + jnp.arange(M, dtype=jnp.int32) - start_sorted[se]
    cap = -(-M // MOE_BLOCK) * MOE_BLOCK + N_EXPERTS * MOE_BLOCK
    n_blk = cap // MOE_BLOCK
    slot_tok = jnp.full((cap,), N, jnp.int32).at[dest].set(stok)
    slot_w = jnp.zeros((cap,), F32).at[dest].set(sw)
    blk_e = jnp.minimum(jnp.searchsorted(end_pad, jnp.arange(n_blk, dtype=jnp.int32) * MOE_BLOCK,
                                         side='right'), N_EXPERTS - 1)
    h_pad = jnp.concatenate([h, jnp.zeros((1, D), h.dtype)], 0)
    xb = h_pad[slot_tok].reshape(n_blk, MOE_BLOCK, D)

    def expert_block(args):
        xe, e = args
        return (jax.nn.silu(xe @ w_gate[e]) * (xe @ w_up[e])) @ w_down[e]

    yb = lax.map(expert_block, (xb, blk_e)).reshape(cap, D)
    out = jnp.zeros((N + 1, D), yb.dtype).at[slot_tok].add(yb * slot_w[:, None].astype(yb.dtype))
    return out[:N]


def setup_inputs(seed: int = 0) -> dict:
    key = jax.random.key(seed)
    ks = iter(jax.random.split(key, 40))
    L, D = DEPTH, D_MODEL

    def nrm(shape, s):
        return jax.random.normal(next(ks), shape, F32) * s

    u = jax.random.uniform(next(ks), (L, 2, RG_WIDTH), F32, minval=0.9, maxval=0.999)
    a0 = u ** (1.0 / RG_C)
    rg_lambda = jnp.log(a0) - jnp.log1p(-a0)
    return {
        'x': nrm((BATCH, SEQ, D), 1.0),
        'c': nrm((BATCH, D), 1.0),
        'ctx': nrm((BATCH, CTX_LEN, D), 1.0),
        'c_ctx': nrm((D,), 1.0),
        'w_ada': nrm((L, D, 6 * D), 0.5 * D ** -0.5),
        'b_ada': nrm((L, 6 * D), 0.02),
        'w_in': nrm((L, D, PROJ_WIDTH), D ** -0.5),
        'conv_w': nrm((L, RG_CONV, RG_WIDTH), RG_CONV ** -0.5),
        'conv_b': nrm((L, RG_WIDTH), 0.02),
        'rg_a_w': nrm((L, 2, RG_BLOCKS, RG_BLOCK, RG_BLOCK), RG_BLOCK ** -0.5),
        'rg_a_b': nrm((L, 2, RG_WIDTH), 0.02),
        'rg_i_w': nrm((L, 2, RG_BLOCKS, RG_BLOCK, RG_BLOCK), RG_BLOCK ** -0.5),
        'rg_i_b': nrm((L, 2, RG_WIDTH), 0.02),
        'rg_lambda': rg_lambda,
        'na_rpb': nrm((L, NA_HEADS, 2 * NA_WIN_R - 1, 2 * NA_WIN_C - 1), 0.1),
        'gqa_sink': nrm((L, GQA_HEADS), 0.5),
        'w_out': nrm((L, MIX_WIDTH, D), DN_BETA * MIX_WIDTH ** -0.5),
        'ln1_g': 1.0 + nrm((L, D), 0.05),
        'ln1_b': nrm((L, D), 0.02),
        'router_g_w': nrm((L, D, N_GROUPS), D ** -0.5),
        'router_g_b': nrm((L, N_GROUPS), 0.01),
        'router_e_w': nrm((L, D, N_EXPERTS), D ** -0.5),
        'router_e_b': nrm((L, N_EXPERTS), 0.01),
        'exp_w_gate': nrm((L, N_EXPERTS, D, EXPERT_FF), D ** -0.5),
        'exp_w_up': nrm((L, N_EXPERTS, D, EXPERT_FF), D ** -0.5),
        'exp_w_down': nrm((L, N_EXPERTS, EXPERT_FF, D), DN_BETA * EXPERT_FF ** -0.5),
        'ln2_g': 1.0 + nrm((L, D), 0.05),
        'ln2_b': nrm((L, D), 0.02),
    }


def reference(x, c, ctx, c_ctx, w_ada, b_ada, w_in, conv_w, conv_b, rg_a_w, rg_a_b, rg_i_w, rg_i_b,
              rg_lambda, na_rpb, gqa_sink, w_out, ln1_g, ln1_b, router_g_w, router_g_b, router_e_w,
              router_e_b, exp_w_gate, exp_w_up, exp_w_down, ln2_g, ln2_b):
    B, S, D = x.shape
    C = ctx.shape[1]
    t = jnp.arange(S, dtype=jnp.int32)
    row, col = t // GRID_W, t % GRID_W
    silu_c = jax.nn.silu(c)
    silu_cc = jax.nn.silu(c_ctx)
    x_lat, x_ctx = x, ctx
    for l in range(DEPTH):
        need_ctx = l < DEPTH - 1
        mod = (silu_c @ w_ada[l] + b_ada[l])[:, None, :]
        mod_c = silu_cc @ w_ada[l] + b_ada[l]
        sh1, sc1, g1, sh2, sc2, g2 = jnp.split(mod, 6, -1)
        csh1, csc1, cg1, csh2, csc2, cg2 = jnp.split(mod_c, 6, -1)
        mix_lat, mix_ctx = mixer(modulate(x_lat, sh1, sc1), modulate(x_ctx, csh1, csc1), row, col,
                                 w_in[l], conv_w[l], conv_b[l], rg_a_w[l], rg_a_b[l], rg_i_w[l], rg_i_b[l],
                                 rg_lambda[l], na_rpb[l], gqa_sink[l], w_out[l], need_ctx)
        x_lat = layer_norm(DN_ALPHA * x_lat + g1 * mix_lat, ln1_g[l], ln1_b[l])
        h2 = modulate(x_lat, sh2, sc2).reshape(B * S, D)
        if need_ctx:
            x_ctx = layer_norm(DN_ALPHA * x_ctx + cg1 * mix_ctx, ln1_g[l], ln1_b[l])
            h2 = jnp.concatenate([h2, modulate(x_ctx, csh2, csc2).reshape(B * C, D)], 0)
        f = hier_moe(h2, router_g_w[l], router_g_b[l], router_e_w[l], router_e_b[l],
                     exp_w_gate[l], exp_w_up[l], exp_w_down[l])
        x_lat = layer_norm(DN_ALPHA * x_lat + g2 * f[:B * S].reshape(B, S, D), ln2_g[l], ln2_b[l])
        if need_ctx:
            x_ctx = layer_norm(DN_ALPHA * x_ctx + cg2 * f[B * S:].reshape(B, C, D), ln2_g[l], ln2_b[l])
    return x_lat
```

```python
import functools
import math

import jax
import jax.numpy as jnp
from jax import lax
from jax.experimental import pallas as pl
from jax.experimental.pallas import tpu as pltpu

F32 = jnp.float32
BF16 = jnp.bfloat16

HEAD_DIM = 64
GRID_W = 64
RG_WIDTH = 256
RG_BLOCKS = 4
RG_CONV = 4
RG_C = 8.0
NA_HEADS = 6
NA_WIDTH = NA_HEADS * HEAD_DIM
NA_WIN_R = 8
NA_WIN_C = 16
GQA_HEADS = 6
GQA_KV_HEADS = 2
GQA_GROUP = GQA_HEADS // GQA_KV_HEADS
GQA_WIDTH = GQA_HEADS * HEAD_DIM
GQA_KV_WIDTH = GQA_KV_HEADS * HEAD_DIM
GQA_WINDOW = 128
GQA_BLOCK = 128
ROPE_BASE = 10000.0
N_GROUPS = 4
EXPERTS_PER_GROUP = 8
N_EXPERTS = N_GROUPS * EXPERTS_PER_GROUP
LN_EPS = 1e-5
ATT_SCALE = HEAD_DIM ** -0.5

_PROJ_SIZES = (RG_WIDTH, NA_WIDTH, NA_WIDTH, GQA_KV_WIDTH, GQA_KV_WIDTH, RG_WIDTH, NA_WIDTH, GQA_WIDTH)
_PROJ_OFFS = tuple(int(sum(_PROJ_SIZES[:i])) for i in range(len(_PROJ_SIZES) + 1))

LANES = 128
NEG = -1e30
VMEM_LIMIT = 56 * 1024 * 1024
MOE_BM = 256
ROUTE_LANES = 128


def _params(sem):
    return pltpu.CompilerParams(dimension_semantics=sem, vmem_limit_bytes=VMEM_LIMIT)


def _ada_kernel(c_ref, w_ref, b_ref, o_ref):
    c = c_ref[...]
    s = c * jax.nn.sigmoid(c)
    o_ref[0] = jnp.dot(s, w_ref[0], preferred_element_type=F32, precision=lax.Precision.HIGHEST) + b_ref[0]


def ada_modulation(cond, w_ada, b_ada):
    L, D, N6 = w_ada.shape
    R = cond.shape[0]
    tn = 1536
    return pl.pallas_call(
        _ada_kernel,
        grid=(L, N6 // tn),
        in_specs=[pl.BlockSpec((R, D), lambda l, j: (0, 0)),
                  pl.BlockSpec((1, D, tn), lambda l, j: (l, 0, j)),
                  pl.BlockSpec((1, 1, tn), lambda l, j: (l, 0, j))],
        out_specs=pl.BlockSpec((1, R, tn), lambda l, j: (l, 0, j)),
        out_shape=jax.ShapeDtypeStruct((L, R, N6), F32),
        compiler_params=_params(("arbitrary", "arbitrary")),
        name="ada_modulation",
    )(cond, w_ada, b_ada.reshape(L, 1, N6))


def _rope(x, cos, sin_signed):
    lane = lax.broadcasted_iota(jnp.int32, x.shape, 1)
    partner = jnp.where(lane % 32 < 16, pltpu.roll(x, LANES - 16, axis=1), pltpu.roll(x, 16, axis=1))
    return x * cos + partner * sin_signed


def _inproj_kernel(x_ref, sc_ref, sh_ref, w_ref, cos_ref, sin_ref,
                   rgx_ref, rgg_ref, nak_ref, nav_ref, naq_ref, gk_ref, gv_ref, gq_ref, *, rope):
    h = (x_ref[...] * sc_ref[0] + sh_ref[0]).astype(BF16)

    def proj(i):
        return jnp.dot(h, w_ref[:, _PROJ_OFFS[i]:_PROJ_OFFS[i + 1]], preferred_element_type=F32)

    rgx_ref[...] = proj(0)
    nak_ref[...] = proj(1).astype(BF16)
    nav_ref[...] = proj(2).astype(BF16)
    gk = proj(3)
    gv_ref[...] = proj(4).astype(BF16)
    rgg_ref[...] = proj(5)
    naq_ref[...] = (proj(6) * ATT_SCALE).astype(BF16)
    gq = proj(7)
    if rope:
        cos = cos_ref[...]
        sin = sin_ref[...]
        gk = _rope(gk, cos, sin)
        gq = jnp.concatenate([_rope(gq[:, i * LANES:(i + 1) * LANES], cos, sin)
                              for i in range(GQA_WIDTH // LANES)], axis=1)
    gk_ref[...] = gk.astype(BF16)
    gq_ref[...] = (gq * ATT_SCALE).astype(BF16)


def in_projection(x2d, sc, sh, w_bf16, cos_t, sin_t, *, seq, rope):
    N, D = x2d.shape
    tm = min(512, seq)
    per = seq // tm
    PW = w_bf16.shape[1]
    outs = [(RG_WIDTH, F32), (RG_WIDTH, F32), (NA_WIDTH, BF16), (NA_WIDTH, BF16), (NA_WIDTH, BF16),
            (GQA_KV_WIDTH, BF16), (GQA_KV_WIDTH, BF16), (GQA_WIDTH, BF16)]
    return pl.pallas_call(
        functools.partial(_inproj_kernel, rope=rope),
        grid=(N // tm,),
        in_specs=[pl.BlockSpec((tm, D), lambda i: (i, 0)),
                  pl.BlockSpec((1, 1, D), lambda i: (i // per, 0, 0)),
                  pl.BlockSpec((1, 1, D), lambda i: (i // per, 0, 0)),
                  pl.BlockSpec((D, PW), lambda i: (0, 0)),
                  pl.BlockSpec((tm, LANES), lambda i: (i % per, 0)),
                  pl.BlockSpec((tm, LANES), lambda i: (i % per, 0))],
        out_specs=[pl.BlockSpec((tm, w), lambda i: (i, 0)) for w, _ in outs],
        out_shape=[jax.ShapeDtypeStruct((N, w), dt) for w, dt in outs],
        compiler_params=_params(("arbitrary",)),
        name="in_projection",
    )(x2d, sc, sh, w_bf16, cos_t, sin_t)


def rope_tables(seq):
    t = jnp.arange(seq, dtype=jnp.int32)
    row, col = t // GRID_W, t % GRID_W
    quarter = HEAD_DIM // 4
    inv = ROPE_BASE ** (-jnp.arange(quarter, dtype=F32) / quarter)
    lane = jnp.arange(LANES)
    pos = jnp.where((lane % HEAD_DIM < HEAD_DIM // 2)[None, :], row[:, None], col[:, None]).astype(F32)
    ang = pos * inv[lane % quarter][None, :]
    sign = jnp.where(lane % (2 * quarter) < quarter, -1.0, 1.0).astype(F32)
    return jnp.cos(ang), jnp.sin(ang) * sign[None, :]


def _rg_kernel(*refs, reverse, combine, n_chunks, tc):
    if combine:
        (x_ref, xp_ref, xn_ref, h0_ref, cw_ref, cb_ref, wa_ref, ba_ref, wi_ref, bi_ref, sp_ref,
         oth_ref, gate_ref, out_ref, hl_ref, xbuf, carry) = refs
    else:
        (x_ref, xp_ref, xn_ref, h0_ref, cw_ref, cb_ref, wa_ref, ba_ref, wi_ref, bi_ref, sp_ref,
         out_ref, hl_ref, xbuf, carry) = refs
    j = pl.program_id(1)
    jt = (n_chunks - 1 - j) if reverse else j

    @pl.when(j == 0)
    def _():
        carry[...] = h0_ref[0]

    xbuf[0:8, :] = jnp.where(jt > 0, xp_ref[0], 0.0)
    xbuf[8:8 + tc, :] = x_ref[0]
    xbuf[8 + tc:16 + tc, :] = jnp.where(jt < n_chunks - 1, xn_ref[0], 0.0)
    xl = cb_ref[...] + sum(cw_ref[k:k + 1, :] * xbuf[pl.ds(8 - RG_CONV // 2 + k, tc), :] for k in range(RG_CONV))

    xb = xl.astype(BF16)
    r = jax.nn.sigmoid(jnp.dot(xb, wa_ref[...], preferred_element_type=F32) + ba_ref[...])
    gi = jax.nn.sigmoid(jnp.dot(xb, wi_ref[...], preferred_element_type=F32) + bi_ref[...])
    log_a = -RG_C * r * sp_ref[...]
    a = jnp.exp(log_a)
    b = jnp.sqrt(-jnp.tanh(log_a) * (a * a + 1.0)) * (gi * xl)

    row = lax.broadcasted_iota(jnp.int32, (tc, RG_WIDTH), 0)
    d = 1
    while d < tc:
        if reverse:
            keep = row < tc - d
            a_n = jnp.where(keep, pltpu.roll(a, tc - d, axis=0), 1.0)
            b_n = jnp.where(keep, pltpu.roll(b, tc - d, axis=0), 0.0)
        else:
            keep = row >= d
            a_n = jnp.where(keep, pltpu.roll(a, d, axis=0), 1.0)
            b_n = jnp.where(keep, pltpu.roll(b, d, axis=0), 0.0)
        b = a * b_n + b
        a = a * a_n
        d *= 2
    h = a * carry[...] + b
    carry[...] = h[0:1, :] if reverse else h[tc - 1:tc, :]
    hl_ref[0] = carry[...]
    if combine:
        out_ref[0] = ((h + oth_ref[0]) * jax.nn.gelu(gate_ref[0])).astype(out_ref.dtype)
    else:
        out_ref[0] = h


def rglru_direction(x, h0, conv_w, conv_b, wa_bd, ba, wi_bd, bi, sp, *, reverse, other=None, gate=None):
    B, T, W = x.shape
    tc = min(256, T)
    n_chunks = T // tc
    combine = other is not None

    def tmap(j):
        return (n_chunks - 1 - j) if reverse else j

    hb = tc // 8
    vec = pl.BlockSpec((1, W), lambda b, j: (0, 0))
    in_specs = [pl.BlockSpec((1, tc, W), lambda b, j: (b, tmap(j), 0)),
                pl.BlockSpec((1, 8, W), lambda b, j: (b, jnp.maximum(tmap(j) * hb - 1, 0), 0)),
                pl.BlockSpec((1, 8, W), lambda b, j: (b, jnp.minimum((tmap(j) + 1) * hb, T // 8 - 1), 0)),
                pl.BlockSpec((1, 1, W), lambda b, j: (b, 0, 0)),
                pl.BlockSpec((RG_CONV, W), lambda b, j: (0, 0)), vec,
                pl.BlockSpec((W, W), lambda b, j: (0, 0)), vec,
                pl.BlockSpec((W, W), lambda b, j: (0, 0)), vec, vec]
    args = [x, x, x, h0, conv_w, conv_b, wa_bd, ba, wi_bd, bi, sp]
    if combine:
        in_specs += [pl.BlockSpec((1, tc, W), lambda b, j: (b, tmap(j), 0))] * 2
        args += [other, gate]
    return pl.pallas_call(
        functools.partial(_rg_kernel, reverse=reverse, combine=combine, n_chunks=n_chunks, tc=tc),
        grid=(B, n_chunks),
        in_specs=in_specs,
        out_specs=[pl.BlockSpec((1, tc, W), lambda b, j: (b, tmap(j), 0)),
                   pl.BlockSpec((1, 1, W), lambda b, j: (b, 0, 0))],
        out_shape=[jax.ShapeDtypeStruct((B, T, W), BF16 if combine else F32),
                   jax.ShapeDtypeStruct((B, 1, W), F32)],
        scratch_shapes=[pltpu.VMEM((tc + 16, W), F32), pltpu.VMEM((1, W), F32)],
        compiler_params=_params(("arbitrary", "arbitrary")),
        name="rglru_bwd" if reverse else "rglru_fwd",
    )(*args)


def _block_diag_dense(w):
    nb, d, _ = w.shape
    eye = jnp.eye(nb, dtype=w.dtype)
    return jnp.einsum('nde,nm->ndme', w, eye).reshape(nb * d, nb * d)


def rglru_mixer(x_lat, x_ctx, gate_lat, gate_ctx, conv_w, conv_b, wa, ba, wi, bi, lam):
    B = x_lat.shape[0]
    W = RG_WIDTH
    sp = jax.nn.softplus(-lam.astype(F32)).reshape(2, 1, W)
    wad = [_block_diag_dense(wa[i]).astype(BF16) for i in range(2)]
    wid = [_block_diag_dense(wi[i]).astype(BF16) for i in range(2)]
    cb = conv_b.reshape(1, W)

    def run(x, h0, i, **kw):
        return rglru_direction(x, h0, conv_w, cb, wad[i], ba[i].reshape(1, W), wid[i], bi[i].reshape(1, W),
                               sp[i], reverse=(i == 1), **kw)

    zeros = jnp.zeros((B, 1, W), F32)
    hc_b, carry_b = run(x_ctx, zeros, 1)
    if gate_ctx is None:
        _, carry_f = run(x_ctx, zeros, 0)
        y_ctx = None
    else:
        y_ctx, carry_f = run(x_ctx, zeros, 0, other=hc_b, gate=gate_ctx)
    hl_b, _ = run(x_lat, carry_b, 1)
    y_lat, _ = run(x_lat, carry_f, 0, other=hl_b, gate=gate_lat)
    return y_lat, y_ctx


def _half_masks(shape):
    lane = lax.broadcasted_iota(jnp.int32, shape, len(shape) - 1)
    return lane < HEAD_DIM, lane >= HEAD_DIM


def _dot_t(a, b):
    return lax.dot_general(a, b, (((1,), (1,)), ((), ())), preferred_element_type=F32)


def _na_kernel(q_ref, k_ref, v_ref, kc_ref, vc_ref, bias_ref, o_ref, *, rows_per_step, n_rows):
    g = pl.program_id(1)
    nwin = NA_WIN_R * GRID_W
    zero = jnp.zeros((), BF16)

    def one_row(i, carry):
        r = g * rows_per_step + i
        rs = jnp.clip(r - NA_WIN_R // 2, 0, n_rows - NA_WIN_R)
        var = r - rs
        start = pl.multiple_of(rs * GRID_W, GRID_W)
        q0 = pl.multiple_of(i * GRID_W, GRID_W)
        for p in range(NA_WIDTH // LANES):
            cs = slice(p * LANES, (p + 1) * LANES)
            qp = q_ref[pl.ds(q0, GRID_W), cs]
            kp = k_ref[0, pl.ds(start, nwin), cs]
            vp = v_ref[0, pl.ds(start, nwin), cs]
            kc = kc_ref[0, :, cs]
            vc = vc_ref[0, :, cs]
            masks_q = _half_masks(qp.shape)
            masks_v = _half_masks(vp.shape)
            masks_c = _half_masks(vc.shape)
            acc = jnp.zeros((GRID_W, LANES), F32)
            for half in range(2):
                qm = jnp.where(masks_q[half], qp, zero)
                s_nb = _dot_t(qm, kp) + bias_ref[var, 2 * p + half]
                s_cx = _dot_t(qm, kc)
                m = jnp.maximum(jnp.max(s_nb, axis=1, keepdims=True), jnp.max(s_cx, axis=1, keepdims=True))
                e_nb = jnp.exp(s_nb - m)
                e_cx = jnp.exp(s_cx - m)
                denom = jnp.sum(e_nb, axis=1, keepdims=True) + jnp.sum(e_cx, axis=1, keepdims=True)
                o = (jnp.dot(e_nb.astype(BF16), jnp.where(masks_v[half], vp, zero), preferred_element_type=F32)
                     + jnp.dot(e_cx.astype(BF16), jnp.where(masks_c[half], vc, zero), preferred_element_type=F32))
                acc = acc + o / denom
            o_ref[pl.ds(q0, GRID_W), cs] = acc.astype(o_ref.dtype)
        return carry

    lax.fori_loop(0, rows_per_step, one_row, 0)


def na_bias_table(rpb):
    cols = jnp.arange(GRID_W)
    cs = jnp.clip(cols - NA_WIN_C // 2, 0, GRID_W - NA_WIN_C)
    valid = (cols[None, :] >= cs[:, None]) & (cols[None, :] < cs[:, None] + NA_WIN_C)
    cb_idx = jnp.clip(cols[None, :] - cols[:, None] + NA_WIN_C - 1, 0, 2 * NA_WIN_C - 2)
    var = jnp.arange(NA_WIN_R)
    rb_idx = jnp.arange(NA_WIN_R)[None, :] - var[:, None] + NA_WIN_R - 1
    t = rpb.astype(F32)[:, rb_idx]
    t = t[:, :, :, cb_idx]
    t = jnp.where(valid[None, None, None], t, NEG)
    t = t.transpose(1, 0, 3, 2, 4)
    return t.reshape(NA_WIN_R, NA_HEADS, GRID_W, NA_WIN_R * GRID_W)


def neighbourhood_attention(q, k, v, k_ctx, v_ctx, bias):
    B, S, Wd = k.shape
    C = k_ctx.shape[1]
    n_rows = S // GRID_W
    rps = 8
    steps = n_rows // rps
    return pl.pallas_call(
        functools.partial(_na_kernel, rows_per_step=rps, n_rows=n_rows),
        grid=(B, steps),
        in_specs=[pl.BlockSpec((rps * GRID_W, Wd), lambda b, g: (b * steps + g, 0)),
                  pl.BlockSpec((1, S, Wd), lambda b, g: (b, 0, 0)),
                  pl.BlockSpec((1, S, Wd), lambda b, g: (b, 0, 0)),
                  pl.BlockSpec((1, C, Wd), lambda b, g: (b, 0, 0)),
                  pl.BlockSpec((1, C, Wd), lambda b, g: (b, 0, 0)),
                  pl.BlockSpec(bias.shape, lambda b, g: (0, 0, 0, 0))],
        out_specs=pl.BlockSpec((rps * GRID_W, Wd), lambda b, g: (b * steps + g, 0)),
        out_shape=jax.ShapeDtypeStruct((B * S, Wd), BF16),
        compiler_params=_params(("arbitrary", "arbitrary")),
        name="neighbourhood_attention",
    )(q, k, v, k_ctx, v_ctx, bias)


def _dup_halves(x):
    lo = x[:, :HEAD_DIM]
    hi = x[:, HEAD_DIM:]
    return jnp.concatenate([lo, lo], axis=1), jnp.concatenate([hi, hi], axis=1)


def _gqa_kernel(sink_ref, q_ref, kp_ref, kcur_ref, kn_ref, vp_ref, vcur_ref, vn_ref, kc_ref, vc_ref, o_ref,
                *, n_blk):
    bi = pl.program_id(1)
    zero = jnp.zeros((), BF16)
    k_loc = jnp.concatenate([kp_ref[0], kcur_ref[0], kn_ref[0]], axis=0)
    v_loc = jnp.concatenate([vp_ref[0], vcur_ref[0], vn_ref[0]], axis=0)
    k_loc_d = _dup_halves(k_loc)
    v_loc_d = _dup_halves(v_loc)
    kc_d = _dup_halves(kc_ref[0])
    vc_d = _dup_halves(vc_ref[0])
    span = 3 * GQA_BLOCK
    qi = lax.broadcasted_iota(jnp.int32, (GQA_BLOCK, span), 0)
    kj = lax.broadcasted_iota(jnp.int32, (GQA_BLOCK, span), 1)
    valid = (jnp.abs(qi + GQA_WINDOW - kj) <= GQA_WINDOW)
    valid &= (kj >= GQA_BLOCK) | (bi > 0)
    valid &= (kj < 2 * GQA_BLOCK) | (bi < n_blk - 1)
    for p in range(GQA_WIDTH // LANES):
        cs = slice(p * LANES, (p + 1) * LANES)
        qp = q_ref[:, cs]
        masks_q = _half_masks(qp.shape)
        masks_v = _half_masks(v_loc.shape)
        masks_c = _half_masks(kc_d[0].shape)
        acc = jnp.zeros((GQA_BLOCK, LANES), F32)
        for half in range(2):
            h = 2 * p + half
            kv = h // GQA_GROUP
            qm = jnp.where(masks_q[half], qp, zero)
            s_loc = jnp.where(valid, _dot_t(qm, k_loc_d[kv]), NEG)
            s_cx = _dot_t(qm, kc_d[kv])
            sink = sink_ref[h]
            m = jnp.maximum(jnp.maximum(jnp.max(s_loc, axis=1, keepdims=True),
                                        jnp.max(s_cx, axis=1, keepdims=True)), sink)
            e_loc = jnp.exp(s_loc - m)
            e_cx = jnp.exp(s_cx - m)
            denom = (jnp.sum(e_loc, axis=1, keepdims=True) + jnp.sum(e_cx, axis=1, keepdims=True)
                     + jnp.exp(sink - m))
            o = (jnp.dot(e_loc.astype(BF16), jnp.where(masks_v[half], v_loc_d[kv], zero),
                         preferred_element_type=F32)
                 + jnp.dot(e_cx.astype(BF16), jnp.where(masks_c[half], vc_d[kv], zero),
                           preferred_element_type=F32))
            acc = acc + o / denom
        o_ref[:, cs] = acc.astype(o_ref.dtype)


def window_gqa(q, k, v, k_ctx, v_ctx, sink):
    B, S, KW = k.shape
    C = k_ctx.shape[1]
    n_blk = S // GQA_BLOCK
    blk = lambda f: pl.BlockSpec((1, GQA_BLOCK, KW), f)
    prev = lambda b, i, s: (b, jnp.maximum(i - 1, 0), 0)
    cur = lambda b, i, s: (b, i, 0)
    nxt = lambda b, i, s: (b, jnp.minimum(i + 1, n_blk - 1), 0)
    ctxs = pl.BlockSpec((1, C, KW), lambda b, i, s: (b, 0, 0))
    grid_spec = pltpu.PrefetchScalarGridSpec(
        num_scalar_prefetch=1,
        grid=(B, n_blk),
        in_specs=[pl.BlockSpec((GQA_BLOCK, GQA_WIDTH), lambda b, i, s: (b * n_blk + i, 0)),
                  blk(prev), blk(cur), blk(nxt), blk(prev), blk(cur), blk(nxt), ctxs, ctxs],
        out_specs=pl.BlockSpec((GQA_BLOCK, GQA_WIDTH), lambda b, i, s: (b * n_blk + i, 0)))
    return pl.pallas_call(
        functools.partial(_gqa_kernel, n_blk=n_blk),
        grid_spec=grid_spec,
        out_shape=jax.ShapeDtypeStruct((B * S, GQA_WIDTH), BF16),
        compiler_params=_params(("arbitrary", "arbitrary")),
        name="window_gqa",
    )(sink.astype(F32), q, k, k, k, v, v, v, k_ctx, v_ctx)


def _ctx_attn_kernel(sink_ref, qn_ref, kn_ref, vn_ref, qg_ref, kg_ref, vg_ref, on_ref, og_ref):
    zero = jnp.zeros((), BF16)
    kg_d = _dup_halves(kg_ref[0])
    vg_d = _dup_halves(vg_ref[0])
    for p in range(NA_WIDTH // LANES):
        cs = slice(p * LANES, (p + 1) * LANES)
        qn = qn_ref[0, :, cs]
        kn = kn_ref[0, :, cs]
        vn = vn_ref[0, :, cs]
        qg = qg_ref[0, :, cs]
        masks = _half_masks(qn.shape)
        acc_n = jnp.zeros(qn.shape, F32)
        acc_g = jnp.zeros(qg.shape, F32)
        for half in range(2):
            h = 2 * p + half
            s = _dot_t(jnp.where(masks[half], qn, zero), kn)
            m = jnp.max(s, axis=1, keepdims=True)
            e = jnp.exp(s - m)
            o = jnp.dot(e.astype(BF16), jnp.where(masks[half], vn, zero), preferred_element_type=F32)
            acc_n = acc_n + o / jnp.sum(e, axis=1, keepdims=True)
            kv = h // GQA_GROUP
            s = _dot_t(jnp.where(masks[half], qg, zero), kg_d[kv])
            sink = sink_ref[h]
            m = jnp.maximum(jnp.max(s, axis=1, keepdims=True), sink)
            e = jnp.exp(s - m)
            o = jnp.dot(e.astype(BF16), jnp.where(masks[half], vg_d[kv], zero), preferred_element_type=F32)
            acc_g = acc_g + o / (jnp.sum(e, axis=1, keepdims=True) + jnp.exp(sink - m))
        on_ref[0, :, cs] = acc_n.astype(on_ref.dtype)
        og_ref[0, :, cs] = acc_g.astype(og_ref.dtype)


def ctx_attention(qn, kn, vn, qg, kg, vg, sink):
    B, C, _ = qn.shape
    spec = lambda w: pl.BlockSpec((1, C, w), lambda b, s: (b, 0, 0))
    grid_spec = pltpu.PrefetchScalarGridSpec(
        num_scalar_prefetch=1, grid=(B,),
        in_specs=[spec(NA_WIDTH), spec(NA_WIDTH), spec(NA_WIDTH), spec(GQA_WIDTH), spec(GQA_KV_WIDTH),
                  spec(GQA_KV_WIDTH)],
        out_specs=[spec(NA_WIDTH), spec(GQA_WIDTH)])
    return pl.pallas_call(
        _ctx_attn_kernel, grid_spec=grid_spec,
        out_shape=[jax.ShapeDtypeStruct((B, C, NA_WIDTH), BF16), jax.ShapeDtypeStruct((B, C, GQA_WIDTH), BF16)],
        compiler_params=_params(("arbitrary",)),
        name="ctx_attention",
    )(sink.astype(F32), qn, kn, vn, qg, kg, vg)


def _layer_norm(y, g, b):
    mu = jnp.mean(y, axis=-1, keepdims=True)
    yc = y - mu
    var = jnp.mean(yc * yc, axis=-1, keepdims=True)
    return yc * lax.rsqrt(var + LN_EPS) * g + b


def _outproj_kernel(yrg_ref, ona_ref, oga_ref, x_ref, w_ref, g1_ref, lg_ref, lb_ref, sc_ref, sh_ref,
                    rwh_ref, rwl_ref, rb_ref, x1_ref, h2_ref, route_ref, *, alpha):
    o0, o1, o2 = RG_WIDTH, RG_WIDTH + NA_WIDTH, RG_WIDTH + NA_WIDTH + GQA_WIDTH
    mix = (jnp.dot(yrg_ref[...], w_ref[0:o0, :], preferred_element_type=F32)
           + jnp.dot(ona_ref[...], w_ref[o0:o1, :], preferred_element_type=F32)
           + jnp.dot(oga_ref[...], w_ref[o1:o2, :], preferred_element_type=F32))
    x1 = _layer_norm(alpha * x_ref[...] + g1_ref[0] * mix, lg_ref[...], lb_ref[...])
    x1_ref[...] = x1
    h2 = x1 * sc_ref[0] + sh_ref[0]
    hi = h2.astype(BF16)
    h2_ref[...] = hi
    lo = (h2 - hi.astype(F32)).astype(BF16)
    logits = (jnp.dot(hi, rwh_ref[...], preferred_element_type=F32)
              + jnp.dot(hi, rwl_ref[...], preferred_element_type=F32)
              + jnp.dot(lo, rwh_ref[...], preferred_element_type=F32)) + rb_ref[...]

    lane = lax.broadcasted_iota(jnp.int32, logits.shape, 1)
    big = jnp.int32(1 << 20)
    gl = jnp.where(lane < N_GROUPS, logits, NEG)
    gmax = jnp.max(gl, axis=1, keepdims=True)
    gsum = jnp.sum(jnp.exp(gl - gmax), axis=1, keepdims=True)
    p_top = 1.0 / gsum
    g_idx = jnp.min(jnp.where(gl == gmax, lane, big), axis=1, keepdims=True)
    lo_lane = N_GROUPS + g_idx * EXPERTS_PER_GROUP
    el = jnp.where((lane >= lo_lane) & (lane < lo_lane + EXPERTS_PER_GROUP), logits, NEG)
    m1 = jnp.max(el, axis=1, keepdims=True)
    i1 = jnp.min(jnp.where(el == m1, lane, big), axis=1, keepdims=True)
    el2 = jnp.where(lane == i1, NEG, el)
    m2 = jnp.max(el2, axis=1, keepdims=True)
    i2 = jnp.min(jnp.where(el2 == m2, lane, big), axis=1, keepdims=True)
    e2 = jnp.exp(m2 - m1)
    w1 = p_top / (1.0 + e2)
    w2 = p_top * e2 / (1.0 + e2)
    route = jnp.where(lane == 0, (i1 - N_GROUPS).astype(F32),
                      jnp.where(lane == 1, (i2 - N_GROUPS).astype(F32),
                                jnp.where(lane == 2, w1, jnp.where(lane == 3, w2, 0.0))))
    route_ref[...] = route


def out_projection(y_rg, o_na, o_ga, x2d, w_out, g1, ln_g, ln_b, sc2, sh2, rw_hi, rw_lo, rb, *, seq, alpha):
    N, D = x2d.shape
    tm = min(512, seq)
    per = seq // tm
    row = lambda w: pl.BlockSpec((tm, w), lambda i: (i, 0))
    per_b = pl.BlockSpec((1, 1, D), lambda i: (i // per, 0, 0))
    vec = pl.BlockSpec((1, D), lambda i: (0, 0))
    full = lambda a: pl.BlockSpec(a.shape, lambda i: (0,) * a.ndim)
    return pl.pallas_call(
        functools.partial(_outproj_kernel, alpha=alpha),
        grid=(N // tm,),
        in_specs=[row(RG_WIDTH), row(NA_WIDTH), row(GQA_WIDTH), row(D), full(w_out), per_b, vec, vec,
                  per_b, per_b, full(rw_hi), full(rw_lo), full(rb)],
        out_specs=[row(D), row(D), row(ROUTE_LANES)],
        out_shape=[jax.ShapeDtypeStruct((N, D), F32), jax.ShapeDtypeStruct((N, D), BF16),
                   jax.ShapeDtypeStruct((N, ROUTE_LANES), F32)],
        compiler_params=_params(("arbitrary",)),
        name="out_projection",
    )(y_rg, o_na, o_ga, x2d, w_out, g1, ln_g, ln_b, sc2, sh2, rw_hi, rw_lo, rb)


def _expert_kernel(be_ref, nu_ref, x_ref, wg_ref, wu_ref, wd_ref, y_ref):
    i = pl.program_id(0)

    @pl.when(i < nu_ref[0])
    def _():
        x = x_ref[...]
        a = jnp.dot(x, wg_ref[0], preferred_element_type=F32)
        u = jnp.dot(x, wu_ref[0], preferred_element_type=F32)
        hmid = (a * jax.nn.sigmoid(a) * u).astype(BF16)
        y_ref[...] = jnp.dot(hmid, wd_ref[0], preferred_element_type=F32).astype(y_ref.dtype)

    @pl.when(i >= nu_ref[0])
    def _():
        y_ref[...] = jnp.zeros(y_ref.shape, y_ref.dtype)


def expert_mlp(xs, blk_e, n_used, w_gate, w_up, w_down):
    cap, D = xs.shape
    FF = w_gate.shape[2]
    n_blk = cap // MOE_BM
    grid_spec = pltpu.PrefetchScalarGridSpec(
        num_scalar_prefetch=2, grid=(n_blk,),
        in_specs=[pl.BlockSpec((MOE_BM, D), lambda i, be, nu: (i, 0)),
                  pl.BlockSpec((1, D, FF), lambda i, be, nu: (be[i], 0, 0)),
                  pl.BlockSpec((1, D, FF), lambda i, be, nu: (be[i], 0, 0)),
                  pl.BlockSpec((1, FF, D), lambda i, be, nu: (be[i], 0, 0))],
        out_specs=pl.BlockSpec((MOE_BM, D), lambda i, be, nu: (i, 0)))
    return pl.pallas_call(
        _expert_kernel, grid_spec=grid_spec,
        out_shape=jax.ShapeDtypeStruct((cap, D), BF16),
        compiler_params=_params(("arbitrary",)),
        name="expert_mlp",
    )(blk_e, n_used, xs, w_gate, w_up, w_down)


def _combine_kernel(x_ref, y0_ref, y1_ref, route_ref, g2_ref, lg_ref, lb_ref, o_ref, *, alpha):
    route = route_ref[...]
    f = route[:, 2:3] * y0_ref[...].astype(F32) + route[:, 3:4] * y1_ref[...].astype(F32)
    o_ref[...] = _layer_norm(alpha * x_ref[...] + g2_ref[0] * f, lg_ref[...], lb_ref[...])


def combine_norm(x1, y0, y1, route, g2, ln_g, ln_b, *, seq, alpha):
    N, D = x1.shape
    tm = min(512, seq)
    per = seq // tm
    row = lambda w: pl.BlockSpec((tm, w), lambda i: (i, 0))
    vec = pl.BlockSpec((1, D), lambda i: (0, 0))
    return pl.pallas_call(
        functools.partial(_combine_kernel, alpha=alpha),
        grid=(N // tm,),
        in_specs=[row(D), row(D), row(D), row(ROUTE_LANES), pl.BlockSpec((1, 1, D), lambda i: (i // per, 0, 0)),
                  vec, vec],
        out_specs=row(D),
        out_shape=jax.ShapeDtypeStruct((N, D), F32),
        compiler_params=_params(("arbitrary",)),
        name="combine_norm",
    )(x1, y0, y1, route, g2, ln_g, ln_b)


def moe_dispatch_indices(expert_ids):
    M = expert_ids.shape[0]
    onehot = (expert_ids[:, None] == jnp.arange(N_EXPERTS, dtype=jnp.int32)[None, :]).astype(jnp.int32)
    csum = jnp.cumsum(onehot, axis=0)
    counts = csum[-1]
    rank = jnp.sum((csum - onehot) * onehot, axis=1)
    padded = (counts + MOE_BM - 1) // MOE_BM * MOE_BM
    end_pad = jnp.cumsum(padded)
    start_pad = end_pad - padded
    dest = start_pad[expert_ids] + rank
    cap = -(-M // MOE_BM) * MOE_BM + N_EXPERTS * MOE_BM
    n_blk = cap // MOE_BM
    slot_src = jnp.zeros((cap,), jnp.int32).at[dest].set(jnp.arange(M, dtype=jnp.int32))
    blk_e = jnp.minimum(jnp.searchsorted(end_pad, jnp.arange(n_blk, dtype=jnp.int32) * MOE_BM, side='right'),
                        N_EXPERTS - 1).astype(jnp.int32)
    n_used = (end_pad[-1] // MOE_BM).astype(jnp.int32).reshape(1)
    return dest, slot_src, blk_e, n_used


def moe(h2_all, route_all, w_gate, w_up, w_down):
    eid = route_all[:, :2].astype(jnp.int32).reshape(-1)
    dest, slot_src, blk_e, n_used = moe_dispatch_indices(eid)
    xs = jnp.take(h2_all, slot_src // 2, axis=0)
    y = expert_mlp(xs, blk_e, n_used, w_gate, w_up, w_down)
    d2 = dest.reshape(-1, 2)
    return jnp.take(y, d2[:, 0], axis=0), jnp.take(y, d2[:, 1], axis=0)


def _split6(m):
    return jnp.split(m, 6, axis=-1)


def kernel(x, c, ctx, c_ctx, w_ada, b_ada, w_in, conv_w, conv_b, rg_a_w, rg_a_b, rg_i_w, rg_i_b, rg_lambda,
           na_rpb, gqa_sink, w_out, ln1_g, ln1_b, router_g_w, router_g_b, router_e_w, router_e_b,
           exp_w_gate, exp_w_up, exp_w_down, ln2_g, ln2_b):
    B, S, D = x.shape
    C = ctx.shape[1]
    depth = w_ada.shape[0]
    alpha = (2 * depth) ** 0.25

    n_cond = -(-(B + 1) // 8) * 8
    cond = jnp.zeros((n_cond, D), F32).at[:B].set(c).at[B].set(c_ctx)
    mod_all = ada_modulation(cond, w_ada, b_ada)
    cos_t, sin_t = rope_tables(S)
    cos_c = jnp.zeros((C, LANES), F32)

    x_lat = x.reshape(B * S, D)
    x_ctx = ctx.reshape(B * C, D)
    for l in range(depth):
        need_ctx = l < depth - 1
        sh1, sc1, g1, sh2, sc2, g2 = [m[:, None, :] for m in _split6(mod_all[l, :B])]
        csh1, csc1, cg1, csh2, csc2, cg2 = [jnp.broadcast_to(m[None, None, :], (B, 1, D))
                                            for m in _split6(mod_all[l, B])]
        w_in_b = w_in[l].astype(BF16)
        w_out_b = w_out[l].astype(BF16)
        rgx, rgg, nak, nav, naq, gk, gv, gq = in_projection(x_lat, 1.0 + sc1, sh1, w_in_b, cos_t, sin_t,
                                                            seq=S, rope=True)
        crgx, crgg, cnak, cnav, cnaq, cgk, cgv, cgq = in_projection(x_ctx, 1.0 + csc1, csh1, w_in_b, cos_c, cos_c,
                                                                    seq=C, rope=False)
        r3 = lambda a, t: a.reshape(B, t, a.shape[-1])
        y_rg, y_rg_c = rglru_mixer(r3(rgx, S), r3(crgx, C), r3(rgg, S), r3(crgg, C) if need_ctx else None,
                                   conv_w[l], conv_b[l], rg_a_w[l], rg_a_b[l], rg_i_w[l], rg_i_b[l], rg_lambda[l])
        o_na = neighbourhood_attention(naq, r3(nak, S), r3(nav, S), r3(cnak, C), r3(cnav, C),
                                       na_bias_table(na_rpb[l]))
        o_ga = window_gqa(gq, r3(gk, S), r3(gv, S), r3(cgk, C), r3(cgv, C), gqa_sink[l])

        rw = jnp.zeros((D, ROUTE_LANES), F32).at[:, :N_GROUPS].set(router_g_w[l])
        rw = rw.at[:, N_GROUPS:N_GROUPS + N_EXPERTS].set(router_e_w[l])
        rw_hi = rw.astype(BF16)
        rw_lo = (rw - rw_hi.astype(F32)).astype(BF16)
        rb = jnp.zeros((1, ROUTE_LANES), F32).at[0, :N_GROUPS].set(router_g_b[l])
        rb = rb.at[0, N_GROUPS:N_GROUPS + N_EXPERTS].set(router_e_b[l])
        lg1, lb1 = ln1_g[l].reshape(1, D), ln1_b[l].reshape(1, D)
        lg2, lb2 = ln2_g[l].reshape(1, D), ln2_b[l].reshape(1, D)

        x1, h2, route = out_projection(y_rg.reshape(B * S, RG_WIDTH), o_na, o_ga, x_lat, w_out_b, g1, lg1, lb1,
                                       1.0 + sc2, sh2, rw_hi, rw_lo, rb, seq=S, alpha=alpha)
        wg_b, wu_b, wd_b = exp_w_gate[l].astype(BF16), exp_w_up[l].astype(BF16), exp_w_down[l].astype(BF16)
        if need_ctx:
            o_na_c, o_ga_c = ctx_attention(r3(cnaq, C), r3(cnak, C), r3(cnav, C), r3(cgq, C), r3(cgk, C),
                                           r3(cgv, C), gqa_sink[l])
            x1c, h2c, route_c = out_projection(y_rg_c.reshape(B * C, RG_WIDTH), o_na_c.reshape(B * C, NA_WIDTH),
                                               o_ga_c.reshape(B * C, GQA_WIDTH), x_ctx, w_out_b, cg1, lg1, lb1,
                                               1.0 + csc2, csh2, rw_hi, rw_lo, rb, seq=C, alpha=alpha)
            y0, y1 = moe(jnp.concatenate([h2, h2c], 0), jnp.concatenate([route, route_c], 0), wg_b, wu_b, wd_b)
            x_ctx = combine_norm(x1c, y0[B * S:], y1[B * S:], route_c, cg2, lg2, lb2, seq=C, alpha=alpha)
        else:
            y0, y1 = moe(h2, route, wg_b, wu_b, wd_b)
        x_lat = combine_norm(x1, y0[:B * S], y1[:B * S], route, g2, lg2, lb2, seq=S, alpha=alpha)
    return x_lat.reshape(B, S, D)
```

```python
import functools
import math

import jax
import jax.numpy as jnp
from jax import lax
from jax.experimental import pallas as pl
from jax.experimental.pallas import tpu as pltpu

F32 = jnp.float32
BF16 = jnp.bfloat16

HEAD_DIM = 64
GRID_W = 64
RG_WIDTH = 256
RG_BLOCKS = 4
RG_CONV = 4
RG_C = 8.0
NA_HEADS = 6
NA_WIDTH = NA_HEADS * HEAD_DIM
NA_WIN_R = 8
NA_WIN_C = 16
GQA_HEADS = 6
GQA_KV_HEADS = 2
GQA_GROUP = GQA_HEADS // GQA_KV_HEADS
GQA_WIDTH = GQA_HEADS * HEAD_DIM
GQA_KV_WIDTH = GQA_KV_HEADS * HEAD_DIM
GQA_WINDOW = 128
GQA_BLOCK = 128
ROPE_BASE = 10000.0
N_GROUPS = 4
EXPERTS_PER_GROUP = 8
N_EXPERTS = N_GROUPS * EXPERTS_PER_GROUP
LN_EPS = 1e-5
ATT_SCALE = HEAD_DIM ** -0.5

_PROJ_SIZES = (RG_WIDTH, NA_WIDTH, NA_WIDTH, GQA_KV_WIDTH, GQA_KV_WIDTH, RG_WIDTH, NA_WIDTH, GQA_WIDTH)
_PROJ_OFFS = tuple(int(sum(_PROJ_SIZES[:i])) for i in range(len(_PROJ_SIZES) + 1))

LANES = 128
NEG = -1e30
VMEM_LIMIT = 56 * 1024 * 1024
MOE_BM = 256
ROUTE_LANES = 128


def _params(sem):
    return pltpu.CompilerParams(dimension_semantics=sem, vmem_limit_bytes=VMEM_LIMIT)


def _ada_kernel(c_ref, w_ref, b_ref, o_ref):
    c = c_ref[...]
    s = c * jax.nn.sigmoid(c)
    o_ref[0] = jnp.dot(s, w_ref[0], preferred_element_type=F32, precision=lax.Precision.HIGHEST) + b_ref[0]


def ada_modulation(cond, w_ada, b_ada):
    L, D, N6 = w_ada.shape
    R = cond.shape[0]
    tn = 1536
    return pl.pallas_call(
        _ada_kernel,
        grid=(L, N6 // tn),
        in_specs=[pl.BlockSpec((R, D), lambda l, j: (0, 0)),
                  pl.BlockSpec((1, D, tn), lambda l, j: (l, 0, j)),
                  pl.BlockSpec((1, 1, tn), lambda l, j: (l, 0, j))],
        out_specs=pl.BlockSpec((1, R, tn), lambda l, j: (l, 0, j)),
        out_shape=jax.ShapeDtypeStruct((L, R, N6), F32),
        compiler_params=_params(("arbitrary", "arbitrary")),
        name="ada_modulation",
    )(cond, w_ada, b_ada.reshape(L, 1, N6))


def _rope(x, cos, sin_signed):
    lane = lax.broadcasted_iota(jnp.int32, x.shape, 1)
    partner = jnp.where(lane % 32 < 16, pltpu.roll(x, LANES - 16, axis=1), pltpu.roll(x, 16, axis=1))
    return x * cos + partner * sin_signed


def _inproj_kernel(x_ref, sc_ref, sh_ref, w_ref, cos_ref, sin_ref,
                   rgx_ref, rgg_ref, nak_ref, nav_ref, naq_ref, gk_ref, gv_ref, gq_ref, *, rope):
    h = (x_ref[...] * sc_ref[0] + sh_ref[0]).astype(BF16)

    def proj(i):
        return jnp.dot(h, w_ref[:, _PROJ_OFFS[i]:_PROJ_OFFS[i + 1]], preferred_element_type=F32)

    rgx_ref[...] = proj(0)
    nak_ref[...] = proj(1).astype(BF16)
    nav_ref[...] = proj(2).astype(BF16)
    gk = proj(3)
    gv_ref[...] = proj(4).astype(BF16)
    rgg_ref[...] = proj(5)
    naq_ref[...] = (proj(6) * ATT_SCALE).astype(BF16)
    gq = proj(7)
    if rope:
        cos = cos_ref[...]
        sin = sin_ref[...]
        gk = _rope(gk, cos, sin)
        gq = jnp.concatenate([_rope(gq[:, i * LANES:(i + 1) * LANES], cos, sin)
                              for i in range(GQA_WIDTH // LANES)], axis=1)
    gk_ref[...] = gk.astype(BF16)
    gq_ref[...] = (gq * ATT_SCALE).astype(BF16)


def in_projection(x2d, sc, sh, w_bf16, cos_t, sin_t, *, seq, rope):
    N, D = x2d.shape
    tm = min(512, seq)
    per = seq // tm
    PW = w_bf16.shape[1]
    outs = [(RG_WIDTH, F32), (RG_WIDTH, F32), (NA_WIDTH, BF16), (NA_WIDTH, BF16), (NA_WIDTH, BF16),
            (GQA_KV_WIDTH, BF16), (GQA_KV_WIDTH, BF16), (GQA_WIDTH, BF16)]
    return pl.pallas_call(
        functools.partial(_inproj_kernel, rope=rope),
        grid=(N // tm,),
        in_specs=[pl.BlockSpec((tm, D), lambda i: (i, 0)),
                  pl.BlockSpec((1, 1, D), lambda i: (i // per, 0, 0)),
                  pl.BlockSpec((1, 1, D), lambda i: (i // per, 0, 0)),
                  pl.BlockSpec((D, PW), lambda i: (0, 0)),
                  pl.BlockSpec((tm, LANES), lambda i: (i % per, 0)),
                  pl.BlockSpec((tm, LANES), lambda i: (i % per, 0))],
        out_specs=[pl.BlockSpec((tm, w), lambda i: (i, 0)) for w, _ in outs],
        out_shape=[jax.ShapeDtypeStruct((N, w), dt) for w, dt in outs],
        compiler_params=_params(("arbitrary",)),
        name="in_projection",
    )(x2d, sc, sh, w_bf16, cos_t, sin_t)


def rope_tables(seq):
    t = jnp.arange(seq, dtype=jnp.int32)
    row, col = t // GRID_W, t % GRID_W
    quarter = HEAD_DIM // 4
    inv = ROPE_BASE ** (-jnp.arange(quarter, dtype=F32) / quarter)
    lane = jnp.arange(LANES)
    pos = jnp.where((lane % HEAD_DIM < HEAD_DIM // 2)[None, :], row[:, None], col[:, None]).astype(F32)
    ang = pos * inv[lane % quarter][None, :]
    sign = jnp.where(lane % (2 * quarter) < quarter, -1.0, 1.0).astype(F32)
    return jnp.cos(ang), jnp.sin(ang) * sign[None, :]


def _rg_kernel(*refs, reverse, combine, n_chunks, tc):
    if combine:
        (x_ref, xp_ref, xn_ref, h0_ref, cw_ref, cb_ref, wa_ref, ba_ref, wi_ref, bi_ref, sp_ref,
         oth_ref, gate_ref, out_ref, hl_ref, xbuf, carry) = refs
    else:
        (x_ref, xp_ref, xn_ref, h0_ref, cw_ref, cb_ref, wa_ref, ba_ref, wi_ref, bi_ref, sp_ref,
         out_ref, hl_ref, xbuf, carry) = refs
    j = pl.program_id(1)
    jt = (n_chunks - 1 - j) if reverse else j

    @pl.when(j == 0)
    def _():
        carry[...] = h0_ref[0]

    xbuf[0:8, :] = jnp.where(jt > 0, xp_ref[0], 0.0)
    xbuf[8:8 + tc, :] = x_ref[0]
    xbuf[8 + tc:16 + tc, :] = jnp.where(jt < n_chunks - 1, xn_ref[0], 0.0)
    xl = cb_ref[...] + sum(cw_ref[k:k + 1, :] * xbuf[pl.ds(8 - RG_CONV // 2 + k, tc), :] for k in range(RG_CONV))

    xb = xl.astype(BF16)
    r = jax.nn.sigmoid(jnp.dot(xb, wa_ref[...], preferred_element_type=F32) + ba_ref[...])
    gi = jax.nn.sigmoid(jnp.dot(xb, wi_ref[...], preferred_element_type=F32) + bi_ref[...])
    log_a = -RG_C * r * sp_ref[...]
    a = jnp.exp(log_a)
    b = jnp.sqrt(-jnp.tanh(log_a) * (a * a + 1.0)) * (gi * xl)

    row = lax.broadcasted_iota(jnp.int32, (tc, RG_WIDTH), 0)
    d = 1
    while d < tc:
        if reverse:
            keep = row < tc - d
            a_n = jnp.where(keep, pltpu.roll(a, tc - d, axis=0), 1.0)
            b_n = jnp.where(keep, pltpu.roll(b, tc - d, axis=0), 0.0)
        else:
            keep = row >= d
            a_n = jnp.where(keep, pltpu.roll(a, d, axis=0), 1.0)
            b_n = jnp.where(keep, pltpu.roll(b, d, axis=0), 0.0)
        b = a * b_n + b
        a = a * a_n
        d *= 2
    h = a * carry[...] + b
    carry[...] = h[0:1, :] if reverse else h[tc - 1:tc, :]
    hl_ref[0] = carry[...]
    if combine:
        out_ref[0] = ((h + oth_ref[0]) * jax.nn.gelu(gate_ref[0])).astype(out_ref.dtype)
    else:
        out_ref[0] = h


def rglru_direction(x, h0, conv_w, conv_b, wa_bd, ba, wi_bd, bi, sp, *, reverse, other=None, gate=None):
    B, T, W = x.shape
    tc = min(256, T)
    n_chunks = T // tc
    combine = other is not None

    def tmap(j):
        return (n_chunks - 1 - j) if reverse else j

    hb = tc // 8
    vec = pl.BlockSpec((1, W), lambda b, j: (0, 0))
    in_specs = [pl.BlockSpec((1, tc, W), lambda b, j: (b, tmap(j), 0)),
                pl.BlockSpec((1, 8, W), lambda b, j: (b, jnp.maximum(tmap(j) * hb - 1, 0), 0)),
                pl.BlockSpec((1, 8, W), lambda b, j: (b, jnp.minimum((tmap(j) + 1) * hb, T // 8 - 1), 0)),
                pl.BlockSpec((1, 1, W), lambda b, j: (b, 0, 0)),
                pl.BlockSpec((RG_CONV, W), lambda b, j: (0, 0)), vec,
                pl.BlockSpec((W, W), lambda b, j: (0, 0)), vec,
                pl.BlockSpec((W, W), lambda b, j: (0, 0)), vec, vec]
    args = [x, x, x, h0, conv_w, conv_b, wa_bd, ba, wi_bd, bi, sp]
    if combine:
        in_specs += [pl.BlockSpec((1, tc, W), lambda b, j: (b, tmap(j), 0))] * 2
        args += [other, gate]
    return pl.pallas_call(
        functools.partial(_rg_kernel, reverse=reverse, combine=combine, n_chunks=n_chunks, tc=tc),
        grid=(B, n_chunks),
        in_specs=in_specs,
        out_specs=[pl.BlockSpec((1, tc, W), lambda b, j: (b, tmap(j), 0)),
                   pl.BlockSpec((1, 1, W), lambda b, j: (b, 0, 0))],
        out_shape=[jax.ShapeDtypeStruct((B, T, W), BF16 if combine else F32),
                   jax.ShapeDtypeStruct((B, 1, W), F32)],
        scratch_shapes=[pltpu.VMEM((tc + 16, W), F32), pltpu.VMEM((1, W), F32)],
        compiler_params=_params(("arbitrary", "arbitrary")),
        name="rglru_bwd" if reverse else "rglru_fwd",
    )(*args)


def _block_diag_dense(w):
    nb, d, _ = w.shape
    eye = jnp.eye(nb, dtype=w.dtype)
    return jnp.einsum('nde,nm->ndme', w, eye).reshape(nb * d, nb * d)


def rglru_mixer(x_lat, x_ctx, gate_lat, gate_ctx, conv_w, conv_b, wa, ba, wi, bi, lam):
    B = x_lat.shape[0]
    W = RG_WIDTH
    sp = jax.nn.softplus(-lam.astype(F32)).reshape(2, 1, W)
    wad = [_block_diag_dense(wa[i]).astype(BF16) for i in range(2)]
    wid = [_block_diag_dense(wi[i]).astype(BF16) for i in range(2)]
    cb = conv_b.reshape(1, W)

    def run(x, h0, i, **kw):
        return rglru_direction(x, h0, conv_w, cb, wad[i], ba[i].reshape(1, W), wid[i], bi[i].reshape(1, W),
                               sp[i], reverse=(i == 1), **kw)

    zeros = jnp.zeros((B, 1, W), F32)
    hc_b, carry_b = run(x_ctx, zeros, 1)
    if gate_ctx is None:
        _, carry_f = run(x_ctx, zeros, 0)
        y_ctx = None
    else:
        y_ctx, carry_f = run(x_ctx, zeros, 0, other=hc_b, gate=gate_ctx)
    hl_b, _ = run(x_lat, carry_b, 1)
    y_lat, _ = run(x_lat, carry_f, 0, other=hl_b, gate=gate_lat)
    return y_lat, y_ctx


def _half_masks(shape):
    lane = lax.broadcasted_iota(jnp.int32, shape, len(shape) - 1)
    return lane < HEAD_DIM, lane >= HEAD_DIM


def _dot_t(a, b):
    return lax.dot_general(a, b, (((1,), (1,)), ((), ())), preferred_element_type=F32)


def _na_kernel(q_ref, k_ref, v_ref, kc_ref, vc_ref, bias_ref, o_ref, *, rows_per_step, n_rows):
    g = pl.program_id(1)
    nwin = NA_WIN_R * GRID_W
    zero = jnp.zeros((), BF16)

    n_pairs = NA_WIDTH // LANES
    lane_lo = lax.broadcasted_iota(jnp.int32, (GRID_W, LANES), 1) < HEAD_DIM

    def one_row(i, carry):
        r = g * rows_per_step + i
        rs = jnp.clip(r - NA_WIN_R // 2, 0, n_rows - NA_WIN_R)
        var = r - rs
        start = pl.multiple_of(rs * GRID_W, GRID_W)
        q0 = pl.multiple_of(i * GRID_W, GRID_W)
        cols = [slice(p * LANES, (p + 1) * LANES) for p in range(n_pairs)]
        scores = []
        for p in range(n_pairs):
            qp = q_ref[pl.ds(q0, GRID_W), cols[p]]
            lhs = jnp.concatenate([jnp.where(lane_lo, qp, zero), jnp.where(lane_lo, zero, qp)], axis=0)
            s_nb = _dot_t(lhs, k_ref[0, pl.ds(start, nwin), cols[p]]) + bias_ref[var, p]
            s_cx = _dot_t(lhs, kc_ref[0, :, cols[p]])
            scores.append((s_nb, s_cx))
        probs = []
        for s_nb, s_cx in scores:
            m = jnp.maximum(jnp.max(s_nb, axis=1, keepdims=True), jnp.max(s_cx, axis=1, keepdims=True))
            e_nb = jnp.exp(s_nb - m)
            e_cx = jnp.exp(s_cx - m)
            denom = jnp.sum(e_nb, axis=1, keepdims=True) + jnp.sum(e_cx, axis=1, keepdims=True)
            probs.append((e_nb.astype(BF16), e_cx.astype(BF16), denom))
        for p in range(n_pairs):
            e_nb, e_cx, denom = probs[p]
            o = (jnp.dot(e_nb, v_ref[0, pl.ds(start, nwin), cols[p]], preferred_element_type=F32)
                 + jnp.dot(e_cx, vc_ref[0, :, cols[p]], preferred_element_type=F32)) / denom
            o_ref[pl.ds(q0, GRID_W), cols[p]] = jnp.where(lane_lo, o[:GRID_W], o[GRID_W:]).astype(o_ref.dtype)
        return carry

    lax.fori_loop(0, rows_per_step, one_row, 0)


def na_bias_table(rpb):
    cols = jnp.arange(GRID_W)
    cs = jnp.clip(cols - NA_WIN_C // 2, 0, GRID_W - NA_WIN_C)
    valid = (cols[None, :] >= cs[:, None]) & (cols[None, :] < cs[:, None] + NA_WIN_C)
    cb_idx = jnp.clip(cols[None, :] - cols[:, None] + NA_WIN_C - 1, 0, 2 * NA_WIN_C - 2)
    var = jnp.arange(NA_WIN_R)
    rb_idx = jnp.arange(NA_WIN_R)[None, :] - var[:, None] + NA_WIN_R - 1
    t = rpb.astype(F32)[:, rb_idx]
    t = t[:, :, :, cb_idx]
    t = jnp.where(valid[None, None, None], t, NEG)
    t = t.transpose(1, 0, 3, 2, 4)
    return t.reshape(NA_WIN_R, NA_HEADS // 2, 2 * GRID_W, NA_WIN_R * GRID_W)


def neighbourhood_attention(q, k, v, k_ctx, v_ctx, bias):
    B, S, Wd = k.shape
    C = k_ctx.shape[1]
    n_rows = S // GRID_W
    rps = 8
    steps = n_rows // rps
    return pl.pallas_call(
        functools.partial(_na_kernel, rows_per_step=rps, n_rows=n_rows),
        grid=(B, steps),
        in_specs=[pl.BlockSpec((rps * GRID_W, Wd), lambda b, g: (b * steps + g, 0)),
                  pl.BlockSpec((1, S, Wd), lambda b, g: (b, 0, 0)),
                  pl.BlockSpec((1, S, Wd), lambda b, g: (b, 0, 0)),
                  pl.BlockSpec((1, C, Wd), lambda b, g: (b, 0, 0)),
                  pl.BlockSpec((1, C, Wd), lambda b, g: (b, 0, 0)),
                  pl.BlockSpec(bias.shape, lambda b, g: (0, 0, 0, 0))],
        out_specs=pl.BlockSpec((rps * GRID_W, Wd), lambda b, g: (b * steps + g, 0)),
        out_shape=jax.ShapeDtypeStruct((B * S, Wd), BF16),
        compiler_params=_params(("arbitrary", "arbitrary")),
        name="neighbourhood_attention",
    )(q, k, v, k_ctx, v_ctx, bias)


def _dup_halves(x):
    lo = x[:, :HEAD_DIM]
    hi = x[:, HEAD_DIM:]
    return jnp.concatenate([lo, lo], axis=1), jnp.concatenate([hi, hi], axis=1)


def _gqa_kernel(sink_ref, q_ref, kp_ref, kcur_ref, kn_ref, vp_ref, vcur_ref, vn_ref, kc_ref, vc_ref, o_ref,
                *, n_blk):
    bi = pl.program_id(1)
    zero = jnp.zeros((), BF16)
    k_loc = jnp.concatenate([kp_ref[0], kcur_ref[0], kn_ref[0]], axis=0)
    v_loc = jnp.concatenate([vp_ref[0], vcur_ref[0], vn_ref[0]], axis=0)
    k_loc_d = _dup_halves(k_loc)
    v_loc_d = _dup_halves(v_loc)
    kc_d = _dup_halves(kc_ref[0])
    vc_d = _dup_halves(vc_ref[0])
    span = 3 * GQA_BLOCK
    rows = GQA_GROUP * GQA_BLOCK
    qi = lax.broadcasted_iota(jnp.int32, (rows, span), 0) % GQA_BLOCK
    kj = lax.broadcasted_iota(jnp.int32, (rows, span), 1)
    valid = (jnp.abs(qi + GQA_WINDOW - kj) <= GQA_WINDOW)
    valid &= (kj >= GQA_BLOCK) | (bi > 0)
    valid &= (kj < 2 * GQA_BLOCK) | (bi < n_blk - 1)
    head_of_row = lax.broadcasted_iota(jnp.int32, (rows, 1), 0) // GQA_BLOCK
    lane_lo = lax.broadcasted_iota(jnp.int32, (GQA_BLOCK, LANES), 1) < HEAD_DIM

    def masked_q(h):
        qp = q_ref[:, (h // 2) * LANES:(h // 2 + 1) * LANES]
        return jnp.where(lane_lo, qp, zero) if h % 2 == 0 else jnp.where(lane_lo, zero, qp)

    scores = []
    for kv in range(GQA_KV_HEADS):
        heads = range(kv * GQA_GROUP, (kv + 1) * GQA_GROUP)
        lhs = jnp.concatenate([masked_q(h) for h in heads], axis=0)
        s_loc = jnp.where(valid, _dot_t(lhs, k_loc_d[kv]), NEG)
        s_cx = _dot_t(lhs, kc_d[kv])
        sink = jnp.zeros((rows, 1), F32)
        for n, h in enumerate(heads):
            sink = jnp.where(head_of_row == n, sink_ref[h], sink)
        scores.append((s_loc, s_cx, sink))
    probs = []
    for s_loc, s_cx, sink in scores:
        m = jnp.maximum(jnp.maximum(jnp.max(s_loc, axis=1, keepdims=True),
                                    jnp.max(s_cx, axis=1, keepdims=True)), sink)
        e_loc = jnp.exp(s_loc - m)
        e_cx = jnp.exp(s_cx - m)
        denom = (jnp.sum(e_loc, axis=1, keepdims=True) + jnp.sum(e_cx, axis=1, keepdims=True)
                 + jnp.exp(sink - m))
        probs.append((e_loc.astype(BF16), e_cx.astype(BF16), denom))
    outs = []
    for kv in range(GQA_KV_HEADS):
        e_loc, e_cx, denom = probs[kv]
        o = (jnp.dot(e_loc, v_loc_d[kv], preferred_element_type=F32)
             + jnp.dot(e_cx, vc_d[kv], preferred_element_type=F32)) / denom
        outs += [o[n * GQA_BLOCK:(n + 1) * GQA_BLOCK] for n in range(GQA_GROUP)]
    for p in range(GQA_WIDTH // LANES):
        o_ref[:, p * LANES:(p + 1) * LANES] = jnp.where(lane_lo, outs[2 * p], outs[2 * p + 1]).astype(o_ref.dtype)


def window_gqa(q, k, v, k_ctx, v_ctx, sink):
    B, S, KW = k.shape
    C = k_ctx.shape[1]
    n_blk = S // GQA_BLOCK
    blk = lambda f: pl.BlockSpec((1, GQA_BLOCK, KW), f)
    prev = lambda b, i, s: (b, jnp.maximum(i - 1, 0), 0)
    cur = lambda b, i, s: (b, i, 0)
    nxt = lambda b, i, s: (b, jnp.minimum(i + 1, n_blk - 1), 0)
    ctxs = pl.BlockSpec((1, C, KW), lambda b, i, s: (b, 0, 0))
    grid_spec = pltpu.PrefetchScalarGridSpec(
        num_scalar_prefetch=1,
        grid=(B, n_blk),
        in_specs=[pl.BlockSpec((GQA_BLOCK, GQA_WIDTH), lambda b, i, s: (b * n_blk + i, 0)),
                  blk(prev), blk(cur), blk(nxt), blk(prev), blk(cur), blk(nxt), ctxs, ctxs],
        out_specs=pl.BlockSpec((GQA_BLOCK, GQA_WIDTH), lambda b, i, s: (b * n_blk + i, 0)))
    return pl.pallas_call(
        functools.partial(_gqa_kernel, n_blk=n_blk),
        grid_spec=grid_spec,
        out_shape=jax.ShapeDtypeStruct((B * S, GQA_WIDTH), BF16),
        compiler_params=_params(("arbitrary", "arbitrary")),
        name="window_gqa",
    )(sink.astype(F32), q, k, k, k, v, v, v, k_ctx, v_ctx)


def _ctx_attn_kernel(sink_ref, qn_ref, kn_ref, vn_ref, qg_ref, kg_ref, vg_ref, on_ref, og_ref):
    zero = jnp.zeros((), BF16)
    kg_d = _dup_halves(kg_ref[0])
    vg_d = _dup_halves(vg_ref[0])
    for p in range(NA_WIDTH // LANES):
        cs = slice(p * LANES, (p + 1) * LANES)
        qn = qn_ref[0, :, cs]
        kn = kn_ref[0, :, cs]
        vn = vn_ref[0, :, cs]
        qg = qg_ref[0, :, cs]
        masks = _half_masks(qn.shape)
        acc_n = jnp.zeros(qn.shape, F32)
        acc_g = jnp.zeros(qg.shape, F32)
        for half in range(2):
            h = 2 * p + half
            s = _dot_t(jnp.where(masks[half], qn, zero), kn)
            m = jnp.max(s, axis=1, keepdims=True)
            e = jnp.exp(s - m)
            o = jnp.dot(e.astype(BF16), jnp.where(masks[half], vn, zero), preferred_element_type=F32)
            acc_n = acc_n + o / jnp.sum(e, axis=1, keepdims=True)
            kv = h // GQA_GROUP
            s = _dot_t(jnp.where(masks[half], qg, zero), kg_d[kv])
            sink = sink_ref[h]
            m = jnp.maximum(jnp.max(s, axis=1, keepdims=True), sink)
            e = jnp.exp(s - m)
            o = jnp.dot(e.astype(BF16), jnp.where(masks[half], vg_d[kv], zero), preferred_element_type=F32)
            acc_g = acc_g + o / (jnp.sum(e, axis=1, keepdims=True) + jnp.exp(sink - m))
        on_ref[0, :, cs] = acc_n.astype(on_ref.dtype)
        og_ref[0, :, cs] = acc_g.astype(og_ref.dtype)


def ctx_attention(qn, kn, vn, qg, kg, vg, sink):
    B, C, _ = qn.shape
    spec = lambda w: pl.BlockSpec((1, C, w), lambda b, s: (b, 0, 0))
    grid_spec = pltpu.PrefetchScalarGridSpec(
        num_scalar_prefetch=1, grid=(B,),
        in_specs=[spec(NA_WIDTH), spec(NA_WIDTH), spec(NA_WIDTH), spec(GQA_WIDTH), spec(GQA_KV_WIDTH),
                  spec(GQA_KV_WIDTH)],
        out_specs=[spec(NA_WIDTH), spec(GQA_WIDTH)])
    return pl.pallas_call(
        _ctx_attn_kernel, grid_spec=grid_spec,
        out_shape=[jax.ShapeDtypeStruct((B, C, NA_WIDTH), BF16), jax.ShapeDtypeStruct((B, C, GQA_WIDTH), BF16)],
        compiler_params=_params(("arbitrary",)),
        name="ctx_attention",
    )(sink.astype(F32), qn, kn, vn, qg, kg, vg)


def _layer_norm(y, g, b):
    mu = jnp.mean(y, axis=-1, keepdims=True)
    yc = y - mu
    var = jnp.mean(yc * yc, axis=-1, keepdims=True)
    return yc * lax.rsqrt(var + LN_EPS) * g + b


def _outproj_kernel(yrg_ref, ona_ref, oga_ref, x_ref, w_ref, g1_ref, lg_ref, lb_ref, sc_ref, sh_ref,
                    rwh_ref, rwl_ref, rb_ref, x1_ref, h2_ref, route_ref, *, alpha):
    o0, o1, o2 = RG_WIDTH, RG_WIDTH + NA_WIDTH, RG_WIDTH + NA_WIDTH + GQA_WIDTH
    mix = (jnp.dot(yrg_ref[...], w_ref[0:o0, :], preferred_element_type=F32)
           + jnp.dot(ona_ref[...], w_ref[o0:o1, :], preferred_element_type=F32)
           + jnp.dot(oga_ref[...], w_ref[o1:o2, :], preferred_element_type=F32))
    x1 = _layer_norm(alpha * x_ref[...] + g1_ref[0] * mix, lg_ref[...], lb_ref[...])
    x1_ref[...] = x1
    h2 = x1 * sc_ref[0] + sh_ref[0]
    hi = h2.astype(BF16)
    h2_ref[...] = hi
    lo = (h2 - hi.astype(F32)).astype(BF16)
    logits = (jnp.dot(hi, rwh_ref[...], preferred_element_type=F32)
              + jnp.dot(hi, rwl_ref[...], preferred_element_type=F32)
              + jnp.dot(lo, rwh_ref[...], preferred_element_type=F32)) + rb_ref[...]

    lane = lax.broadcasted_iota(jnp.int32, logits.shape, 1)
    big = jnp.int32(1 << 20)
    gl = jnp.where(lane < N_GROUPS, logits, NEG)
    gmax = jnp.max(gl, axis=1, keepdims=True)
    gsum = jnp.sum(jnp.exp(gl - gmax), axis=1, keepdims=True)
    p_top = 1.0 / gsum
    g_idx = jnp.min(jnp.where(gl == gmax, lane, big), axis=1, keepdims=True)
    lo_lane = N_GROUPS + g_idx * EXPERTS_PER_GROUP
    el = jnp.where((lane >= lo_lane) & (lane < lo_lane + EXPERTS_PER_GROUP), logits, NEG)
    m1 = jnp.max(el, axis=1, keepdims=True)
    i1 = jnp.min(jnp.where(el == m1, lane, big), axis=1, keepdims=True)
    el2 = jnp.where(lane == i1, NEG, el)
    m2 = jnp.max(el2, axis=1, keepdims=True)
    i2 = jnp.min(jnp.where(el2 == m2, lane, big), axis=1, keepdims=True)
    e2 = jnp.exp(m2 - m1)
    w1 = p_top / (1.0 + e2)
    w2 = p_top * e2 / (1.0 + e2)
    route = jnp.where(lane == 0, (i1 - N_GROUPS).astype(F32),
                      jnp.where(lane == 1, (i2 - N_GROUPS).astype(F32),
                                jnp.where(lane == 2, w1, jnp.where(lane == 3, w2, 0.0))))
    route_ref[...] = route


def _outproj_kernel_shared(*refs, alpha):
    _outproj_kernel(*refs[:13], *refs[15:], alpha=alpha)


def out_projection(y_rg, o_na, o_ga, x2d, w_out, g1, ln_g, ln_b, sc2, sh2, rw_hi, rw_lo, rb, *, seq, alpha,
                   n_total, row_off=0, shared=None):
    N, D = x2d.shape
    tm = min(512, seq)
    per = seq // tm
    off = row_off // tm
    row = lambda w: pl.BlockSpec((tm, w), lambda i: (i, 0))
    row_sh = lambda w: pl.BlockSpec((tm, w), lambda i: (i + off, 0))
    per_b = pl.BlockSpec((1, 1, D), lambda i: (i // per, 0, 0))
    vec = pl.BlockSpec((1, D), lambda i: (0, 0))
    full = lambda a: pl.BlockSpec(a.shape, lambda i: (0,) * a.ndim)
    in_specs = [row(RG_WIDTH), row(NA_WIDTH), row(GQA_WIDTH), row(D), full(w_out), per_b, vec, vec,
                per_b, per_b, full(rw_hi), full(rw_lo), full(rb)]
    args = [y_rg, o_na, o_ga, x2d, w_out, g1, ln_g, ln_b, sc2, sh2, rw_hi, rw_lo, rb]
    aliases = {}
    body = _outproj_kernel
    if shared is not None:
        in_specs += [pl.BlockSpec(memory_space=pl.ANY)] * 2
        args += list(shared)
        aliases = {13: 1, 14: 2}
        body = _outproj_kernel_shared
    return pl.pallas_call(
        functools.partial(body, alpha=alpha),
        grid=(N // tm,),
        in_specs=in_specs,
        out_specs=[row(D), row_sh(D), row_sh(ROUTE_LANES)],
        out_shape=[jax.ShapeDtypeStruct((N, D), F32), jax.ShapeDtypeStruct((n_total, D), BF16),
                   jax.ShapeDtypeStruct((n_total, ROUTE_LANES), F32)],
        input_output_aliases=aliases,
        compiler_params=_params(("arbitrary",)),
        name="out_projection",
    )(*args)


def _expert_kernel(be_ref, nu_ref, x_ref, wg_ref, wu_ref, wd_ref, y_ref, wg_s, wu_s, wd_s):
    i = pl.program_id(0)

    @pl.when((i == 0) | (be_ref[i] != be_ref[jnp.maximum(i - 1, 0)]))
    def _():
        wg_s[...] = wg_ref[0].astype(BF16)
        wu_s[...] = wu_ref[0].astype(BF16)
        wd_s[...] = wd_ref[0].astype(BF16)

    @pl.when(i < nu_ref[0])
    def _():
        x = x_ref[...]
        a = jnp.dot(x, wg_s[...], preferred_element_type=F32)
        u = jnp.dot(x, wu_s[...], preferred_element_type=F32)
        hmid = (a * jax.nn.sigmoid(a) * u).astype(BF16)
        y_ref[...] = jnp.dot(hmid, wd_s[...], preferred_element_type=F32).astype(y_ref.dtype)

    @pl.when(i >= nu_ref[0])
    def _():
        y_ref[...] = jnp.zeros(y_ref.shape, y_ref.dtype)


def expert_mlp(xs, blk_e, n_used, w_gate, w_up, w_down):
    cap, D = xs.shape
    FF = w_gate.shape[2]
    n_blk = cap // MOE_BM
    grid_spec = pltpu.PrefetchScalarGridSpec(
        num_scalar_prefetch=2, grid=(n_blk,),
        in_specs=[pl.BlockSpec((MOE_BM, D), lambda i, be, nu: (i, 0)),
                  pl.BlockSpec((1, D, FF), lambda i, be, nu: (be[i], 0, 0)),
                  pl.BlockSpec((1, D, FF), lambda i, be, nu: (be[i], 0, 0)),
                  pl.BlockSpec((1, FF, D), lambda i, be, nu: (be[i], 0, 0))],
        out_specs=pl.BlockSpec((MOE_BM, D), lambda i, be, nu: (i, 0)),
        scratch_shapes=[pltpu.VMEM((D, FF), BF16), pltpu.VMEM((D, FF), BF16), pltpu.VMEM((FF, D), BF16)])
    return pl.pallas_call(
        _expert_kernel, grid_spec=grid_spec,
        out_shape=jax.ShapeDtypeStruct((cap, D), BF16),
        compiler_params=_params(("arbitrary",)),
        name="expert_mlp",
    )(blk_e, n_used, xs, w_gate, w_up, w_down)


def _combine_kernel(x_ref, y0_ref, y1_ref, route_ref, g2_ref, lg_ref, lb_ref, o_ref, *, alpha):
    route = route_ref[...]
    f = route[:, 2:3] * y0_ref[...].astype(F32) + route[:, 3:4] * y1_ref[...].astype(F32)
    o_ref[...] = _layer_norm(alpha * x_ref[...] + g2_ref[0] * f, lg_ref[...], lb_ref[...])


def combine_norm(x1, y0, y1, route, g2, ln_g, ln_b, *, seq, alpha, row_off=0):
    N, D = x1.shape
    tm = min(512, seq)
    per = seq // tm
    off = row_off // tm
    row = lambda w: pl.BlockSpec((tm, w), lambda i: (i, 0))
    vec = pl.BlockSpec((1, D), lambda i: (0, 0))
    return pl.pallas_call(
        functools.partial(_combine_kernel, alpha=alpha),
        grid=(N // tm,),
        in_specs=[row(D), row(D), row(D), pl.BlockSpec((tm, ROUTE_LANES), lambda i: (i + off, 0)),
                  pl.BlockSpec((1, 1, D), lambda i: (i // per, 0, 0)), vec, vec],
        out_specs=row(D),
        out_shape=jax.ShapeDtypeStruct((N, D), F32),
        compiler_params=_params(("arbitrary",)),
        name="combine_norm",
    )(x1, y0, y1, route, g2, ln_g, ln_b)


def moe_dispatch_indices(expert_ids):
    M = expert_ids.shape[0]
    onehot = (expert_ids[:, None] == jnp.arange(N_EXPERTS, dtype=jnp.int32)[None, :]).astype(jnp.int32)
    csum = jnp.cumsum(onehot, axis=0)
    counts = csum[-1]
    rank = jnp.sum((csum - onehot) * onehot, axis=1)
    padded = (counts + MOE_BM - 1) // MOE_BM * MOE_BM
    end_pad = jnp.cumsum(padded)
    start_pad = end_pad - padded
    dest = start_pad[expert_ids] + rank
    cap = -(-M // MOE_BM) * MOE_BM + N_EXPERTS * MOE_BM
    n_blk = cap // MOE_BM
    slot_src = jnp.zeros((cap,), jnp.int32).at[dest].set(jnp.arange(M, dtype=jnp.int32))
    blk_start = jnp.arange(n_blk, dtype=jnp.int32) * MOE_BM
    blk_e = jnp.minimum(jnp.sum((end_pad[None, :] <= blk_start[:, None]).astype(jnp.int32), axis=1), N_EXPERTS - 1)
    n_used = (end_pad[-1] // MOE_BM).astype(jnp.int32).reshape(1)
    return dest, slot_src, blk_e, n_used


def moe(h2_all, route_all, w_gate, w_up, w_down, n_lat):
    eid = route_all[:, :2].astype(jnp.int32).reshape(-1)
    dest, slot_src, blk_e, n_used = moe_dispatch_indices(eid)
    xs = jnp.take(h2_all, slot_src // 2, axis=0, mode="clip")
    y = expert_mlp(xs, blk_e, n_used, w_gate, w_up, w_down)
    d2 = dest.reshape(-1, 2)
    pick = lambda idx: jnp.take(y, idx, axis=0, mode="clip")
    lat = (pick(d2[:n_lat, 0]), pick(d2[:n_lat, 1]))
    if n_lat == d2.shape[0]:
        return lat, None
    return lat, (pick(d2[n_lat:, 0]), pick(d2[n_lat:, 1]))


def _split6(m):
    return jnp.split(m, 6, axis=-1)


def kernel(x, c, ctx, c_ctx, w_ada, b_ada, w_in, conv_w, conv_b, rg_a_w, rg_a_b, rg_i_w, rg_i_b, rg_lambda,
           na_rpb, gqa_sink, w_out, ln1_g, ln1_b, router_g_w, router_g_b, router_e_w, router_e_b,
           exp_w_gate, exp_w_up, exp_w_down, ln2_g, ln2_b):
    B, S, D = x.shape
    C = ctx.shape[1]
    depth = w_ada.shape[0]
    alpha = (2 * depth) ** 0.25

    n_cond = -(-(B + 1) // 8) * 8
    cond = jnp.zeros((n_cond, D), F32).at[:B].set(c).at[B].set(c_ctx)
    mod_all = ada_modulation(cond, w_ada, b_ada)
    cos_t, sin_t = rope_tables(S)
    cos_c = jnp.zeros((C, LANES), F32)

    x_lat = x.reshape(B * S, D)
    x_ctx = ctx.reshape(B * C, D)
    for l in range(depth):
        need_ctx = l < depth - 1
        sh1, sc1, g1, sh2, sc2, g2 = [m[:, None, :] for m in _split6(mod_all[l, :B])]
        csh1, csc1, cg1, csh2, csc2, cg2 = [jnp.broadcast_to(m[None, None, :], (B, 1, D))
                                            for m in _split6(mod_all[l, B])]
        w_in_b = w_in[l].astype(BF16)
        w_out_b = w_out[l].astype(BF16)
        rgx, rgg, nak, nav, naq, gk, gv, gq = in_projection(x_lat, 1.0 + sc1, sh1, w_in_b, cos_t, sin_t,
                                                            seq=S, rope=True)
        crgx, crgg, cnak, cnav, cnaq, cgk, cgv, cgq = in_projection(x_ctx, 1.0 + csc1, csh1, w_in_b, cos_c, cos_c,
                                                                    seq=C, rope=False)
        r3 = lambda a, t: a.reshape(B, t, a.shape[-1])
        y_rg, y_rg_c = rglru_mixer(r3(rgx, S), r3(crgx, C), r3(rgg, S), r3(crgg, C) if need_ctx else None,
                                   conv_w[l], conv_b[l], rg_a_w[l], rg_a_b[l], rg_i_w[l], rg_i_b[l], rg_lambda[l])
        o_na = neighbourhood_attention(naq, r3(nak, S), r3(nav, S), r3(cnak, C), r3(cnav, C),
                                       na_bias_table(na_rpb[l]))
        o_ga = window_gqa(gq, r3(gk, S), r3(gv, S), r3(cgk, C), r3(cgv, C), gqa_sink[l])

        rw = jnp.zeros((D, ROUTE_LANES), F32).at[:, :N_GROUPS].set(router_g_w[l])
        rw = rw.at[:, N_GROUPS:N_GROUPS + N_EXPERTS].set(router_e_w[l])
        rw_hi = rw.astype(BF16)
        rw_lo = (rw - rw_hi.astype(F32)).astype(BF16)
        rb = jnp.zeros((1, ROUTE_LANES), F32).at[0, :N_GROUPS].set(router_g_b[l])
        rb = rb.at[0, N_GROUPS:N_GROUPS + N_EXPERTS].set(router_e_b[l])
        lg1, lb1 = ln1_g[l].reshape(1, D), ln1_b[l].reshape(1, D)
        lg2, lb2 = ln2_g[l].reshape(1, D), ln2_b[l].reshape(1, D)

        n_lat = B * S
        n_total = n_lat + (B * C if need_ctx else 0)
        x1, h2, route = out_projection(y_rg.reshape(n_lat, RG_WIDTH), o_na, o_ga, x_lat, w_out_b, g1, lg1, lb1,
                                       1.0 + sc2, sh2, rw_hi, rw_lo, rb, seq=S, alpha=alpha, n_total=n_total)
        if need_ctx:
            o_na_c, o_ga_c = ctx_attention(r3(cnaq, C), r3(cnak, C), r3(cnav, C), r3(cgq, C), r3(cgk, C),
                                           r3(cgv, C), gqa_sink[l])
            x1c, h2, route = out_projection(y_rg_c.reshape(B * C, RG_WIDTH), o_na_c.reshape(B * C, NA_WIDTH),
                                            o_ga_c.reshape(B * C, GQA_WIDTH), x_ctx, w_out_b, cg1, lg1, lb1,
                                            1.0 + csc2, csh2, rw_hi, rw_lo, rb, seq=C, alpha=alpha,
                                            n_total=n_total, row_off=n_lat, shared=(h2, route))
        y_lat, y_ctx = moe(h2, route, exp_w_gate[l], exp_w_up[l], exp_w_down[l], n_lat)
        if need_ctx:
            x_ctx = combine_norm(x1c, y_ctx[0], y_ctx[1], route, cg2, lg2, lb2, seq=C, alpha=alpha, row_off=n_lat)
        x_lat = combine_norm(x1, y_lat[0], y_lat[1], route, g2, lg2, lb2, seq=S, alpha=alpha)
    return x_lat.reshape(B, S, D)
```

```python
import functools
import math

import jax
import jax.numpy as jnp
from jax import lax
from jax.experimental import pallas as pl
from jax.experimental.pallas import tpu as pltpu

F32 = jnp.float32
BF16 = jnp.bfloat16

HEAD_DIM = 64
GRID_W = 64
RG_WIDTH = 256
RG_BLOCKS = 4
RG_CONV = 4
RG_C = 8.0
NA_HEADS = 6
NA_WIDTH = NA_HEADS * HEAD_DIM
NA_WIN_R = 8
NA_WIN_C = 16
GQA_HEADS = 6
GQA_KV_HEADS = 2
GQA_GROUP = GQA_HEADS // GQA_KV_HEADS
GQA_WIDTH = GQA_HEADS * HEAD_DIM
GQA_KV_WIDTH = GQA_KV_HEADS * HEAD_DIM
GQA_WINDOW = 128
GQA_BLOCK = 128
ROPE_BASE = 10000.0
N_GROUPS = 4
EXPERTS_PER_GROUP = 8
N_EXPERTS = N_GROUPS * EXPERTS_PER_GROUP
LN_EPS = 1e-5
ATT_SCALE = HEAD_DIM ** -0.5

_PROJ_SIZES = (RG_WIDTH, NA_WIDTH, NA_WIDTH, GQA_KV_WIDTH, GQA_KV_WIDTH, RG_WIDTH, NA_WIDTH, GQA_WIDTH)
_PROJ_OFFS = tuple(int(sum(_PROJ_SIZES[:i])) for i in range(len(_PROJ_SIZES) + 1))

LANES = 128
NEG = -1e30
VMEM_LIMIT = 56 * 1024 * 1024
MOE_BM = 256
ROUTE_LANES = 128
SUBLANES = 8
ROW_TILE = SUBLANES * LANES
COMBINE_TM = 256


def _params(sem):
    return pltpu.CompilerParams(dimension_semantics=sem, vmem_limit_bytes=VMEM_LIMIT)


def _ada_kernel(c_ref, w_ref, b_ref, o_ref):
    c = c_ref[...]
    s = c * jax.nn.sigmoid(c)
    o_ref[0] = jnp.dot(s, w_ref[0], preferred_element_type=F32, precision=lax.Precision.HIGHEST) + b_ref[0]


def ada_modulation(cond, w_ada, b_ada):
    L, D, N6 = w_ada.shape
    R = cond.shape[0]
    tn = 1536
    return pl.pallas_call(
        _ada_kernel,
        grid=(L, N6 // tn),
        in_specs=[pl.BlockSpec((R, D), lambda l, j: (0, 0)),
                  pl.BlockSpec((1, D, tn), lambda l, j: (l, 0, j)),
                  pl.BlockSpec((1, 1, tn), lambda l, j: (l, 0, j))],
        out_specs=pl.BlockSpec((1, R, tn), lambda l, j: (l, 0, j)),
        out_shape=jax.ShapeDtypeStruct((L, R, N6), F32),
        compiler_params=_params(("arbitrary", "arbitrary")),
        name="ada_modulation",
    )(cond, w_ada, b_ada.reshape(L, 1, N6))


def _rope(x, cos, sin_signed):
    lane = lax.broadcasted_iota(jnp.int32, x.shape, 1)
    partner = jnp.where(lane % 32 < 16, pltpu.roll(x, LANES - 16, axis=1), pltpu.roll(x, 16, axis=1))
    return x * cos + partner * sin_signed


def _inproj_kernel(x_ref, sc_ref, sh_ref, w_ref, cos_ref, sin_ref,
                   rgx_ref, rgg_ref, nak_ref, nav_ref, naq_ref, gk_ref, gv_ref, gq_ref, *, rope):
    h = (x_ref[...] * sc_ref[0] + sh_ref[0]).astype(BF16)

    def proj(i):
        return jnp.dot(h, w_ref[:, _PROJ_OFFS[i]:_PROJ_OFFS[i + 1]], preferred_element_type=F32)

    rgx_ref[...] = proj(0)
    nak_ref[...] = proj(1).astype(BF16)
    nav_ref[...] = proj(2).astype(BF16)
    gk = proj(3)
    gv_ref[...] = proj(4).astype(BF16)
    rgg_ref[...] = proj(5)
    naq_ref[...] = (proj(6) * ATT_SCALE).astype(BF16)
    gq = proj(7)
    if rope:
        cos = cos_ref[...]
        sin = sin_ref[...]
        gk = _rope(gk, cos, sin)
        gq = jnp.concatenate([_rope(gq[:, i * LANES:(i + 1) * LANES], cos, sin)
                              for i in range(GQA_WIDTH // LANES)], axis=1)
    gk_ref[...] = gk.astype(BF16)
    gq_ref[...] = (gq * ATT_SCALE).astype(BF16)


def in_projection(x2d, sc, sh, w_bf16, cos_t, sin_t, *, seq, rope):
    N, D = x2d.shape
    tm = min(512, seq)
    per = seq // tm
    PW = w_bf16.shape[1]
    outs = [(RG_WIDTH, F32), (RG_WIDTH, F32), (NA_WIDTH, BF16), (NA_WIDTH, BF16), (NA_WIDTH, BF16),
            (GQA_KV_WIDTH, BF16), (GQA_KV_WIDTH, BF16), (GQA_WIDTH, BF16)]
    return pl.pallas_call(
        functools.partial(_inproj_kernel, rope=rope),
        grid=(N // tm,),
        in_specs=[pl.BlockSpec((tm, D), lambda i: (i, 0)),
                  pl.BlockSpec((1, 1, D), lambda i: (i // per, 0, 0)),
                  pl.BlockSpec((1, 1, D), lambda i: (i // per, 0, 0)),
                  pl.BlockSpec((D, PW), lambda i: (0, 0)),
                  pl.BlockSpec((tm, LANES), lambda i: (i % per, 0)),
                  pl.BlockSpec((tm, LANES), lambda i: (i % per, 0))],
        out_specs=[pl.BlockSpec((tm, w), lambda i: (i, 0)) for w, _ in outs],
        out_shape=[jax.ShapeDtypeStruct((N, w), dt) for w, dt in outs],
        compiler_params=_params(("arbitrary",)),
        name="in_projection",
    )(x2d, sc, sh, w_bf16, cos_t, sin_t)


def rope_tables(seq):
    t = jnp.arange(seq, dtype=jnp.int32)
    row, col = t // GRID_W, t % GRID_W
    quarter = HEAD_DIM // 4
    inv = ROPE_BASE ** (-jnp.arange(quarter, dtype=F32) / quarter)
    lane = jnp.arange(LANES)
    pos = jnp.where((lane % HEAD_DIM < HEAD_DIM // 2)[None, :], row[:, None], col[:, None]).astype(F32)
    ang = pos * inv[lane % quarter][None, :]
    sign = jnp.where(lane % (2 * quarter) < quarter, -1.0, 1.0).astype(F32)
    return jnp.cos(ang), jnp.sin(ang) * sign[None, :]


def _rg_kernel(*refs, reverse, combine, n_chunks, tc):
    if combine:
        (x_ref, xp_ref, xn_ref, h0_ref, cw_ref, cb_ref, wa_ref, ba_ref, wi_ref, bi_ref, sp_ref,
         oth_ref, gate_ref, out_ref, hl_ref, xbuf, carry) = refs
    else:
        (x_ref, xp_ref, xn_ref, h0_ref, cw_ref, cb_ref, wa_ref, ba_ref, wi_ref, bi_ref, sp_ref,
         out_ref, hl_ref, xbuf, carry) = refs
    j = pl.program_id(1)
    jt = (n_chunks - 1 - j) if reverse else j

    @pl.when(j == 0)
    def _():
        carry[...] = h0_ref[0]

    xbuf[0:8, :] = jnp.where(jt > 0, xp_ref[0], 0.0)
    xbuf[8:8 + tc, :] = x_ref[0]
    xbuf[8 + tc:16 + tc, :] = jnp.where(jt < n_chunks - 1, xn_ref[0], 0.0)
    xl = cb_ref[...] + sum(cw_ref[k:k + 1, :] * xbuf[pl.ds(8 - RG_CONV // 2 + k, tc), :] for k in range(RG_CONV))

    xb = xl.astype(BF16)
    r = jax.nn.sigmoid(jnp.dot(xb, wa_ref[...], preferred_element_type=F32) + ba_ref[...])
    gi = jax.nn.sigmoid(jnp.dot(xb, wi_ref[...], preferred_element_type=F32) + bi_ref[...])
    log_a = -RG_C * r * sp_ref[...]
    a = jnp.exp(log_a)
    b = jnp.sqrt(-jnp.tanh(log_a) * (a * a + 1.0)) * (gi * xl)

    row = lax.broadcasted_iota(jnp.int32, (tc, RG_WIDTH), 0)
    d = 1
    while d < tc:
        if reverse:
            keep = row < tc - d
            a_n = jnp.where(keep, pltpu.roll(a, tc - d, axis=0), 1.0)
            b_n = jnp.where(keep, pltpu.roll(b, tc - d, axis=0), 0.0)
        else:
            keep = row >= d
            a_n = jnp.where(keep, pltpu.roll(a, d, axis=0), 1.0)
            b_n = jnp.where(keep, pltpu.roll(b, d, axis=0), 0.0)
        b = a * b_n + b
        a = a * a_n
        d *= 2
    h = a * carry[...] + b
    carry[...] = h[0:1, :] if reverse else h[tc - 1:tc, :]
    hl_ref[0] = carry[...]
    if combine:
        out_ref[0] = ((h + oth_ref[0]) * jax.nn.gelu(gate_ref[0])).astype(out_ref.dtype)
    else:
        out_ref[0] = h


def rglru_direction(x, h0, conv_w, conv_b, wa_bd, ba, wi_bd, bi, sp, *, reverse, other=None, gate=None):
    B, T, W = x.shape
    tc = min(256, T)
    n_chunks = T // tc
    combine = other is not None

    def tmap(j):
        return (n_chunks - 1 - j) if reverse else j

    hb = tc // 8
    vec = pl.BlockSpec((1, W), lambda b, j: (0, 0))
    in_specs = [pl.BlockSpec((1, tc, W), lambda b, j: (b, tmap(j), 0)),
                pl.BlockSpec((1, 8, W), lambda b, j: (b, jnp.maximum(tmap(j) * hb - 1, 0), 0)),
                pl.BlockSpec((1, 8, W), lambda b, j: (b, jnp.minimum((tmap(j) + 1) * hb, T // 8 - 1), 0)),
                pl.BlockSpec((1, 1, W), lambda b, j: (b, 0, 0)),
                pl.BlockSpec((RG_CONV, W), lambda b, j: (0, 0)), vec,
                pl.BlockSpec((W, W), lambda b, j: (0, 0)), vec,
                pl.BlockSpec((W, W), lambda b, j: (0, 0)), vec, vec]
    args = [x, x, x, h0, conv_w, conv_b, wa_bd, ba, wi_bd, bi, sp]
    if combine:
        in_specs += [pl.BlockSpec((1, tc, W), lambda b, j: (b, tmap(j), 0))] * 2
        args += [other, gate]
    return pl.pallas_call(
        functools.partial(_rg_kernel, reverse=reverse, combine=combine, n_chunks=n_chunks, tc=tc),
        grid=(B, n_chunks),
        in_specs=in_specs,
        out_specs=[pl.BlockSpec((1, tc, W), lambda b, j: (b, tmap(j), 0)),
                   pl.BlockSpec((1, 1, W), lambda b, j: (b, 0, 0))],
        out_shape=[jax.ShapeDtypeStruct((B, T, W), BF16 if combine else F32),
                   jax.ShapeDtypeStruct((B, 1, W), F32)],
        scratch_shapes=[pltpu.VMEM((tc + 16, W), F32), pltpu.VMEM((1, W), F32)],
        compiler_params=_params(("arbitrary", "arbitrary")),
        name="rglru_bwd" if reverse else "rglru_fwd",
    )(*args)


def _block_diag_dense(w):
    nb, d, _ = w.shape
    eye = jnp.eye(nb, dtype=w.dtype)
    return jnp.einsum('nde,nm->ndme', w, eye).reshape(nb * d, nb * d)


def rglru_mixer(x_lat, x_ctx, gate_lat, gate_ctx, conv_w, conv_b, wa, ba, wi, bi, lam):
    B = x_lat.shape[0]
    W = RG_WIDTH
    sp = jax.nn.softplus(-lam.astype(F32)).reshape(2, 1, W)
    wad = [_block_diag_dense(wa[i]).astype(BF16) for i in range(2)]
    wid = [_block_diag_dense(wi[i]).astype(BF16) for i in range(2)]
    cb = conv_b.reshape(1, W)

    def run(x, h0, i, **kw):
        return rglru_direction(x, h0, conv_w, cb, wad[i], ba[i].reshape(1, W), wid[i], bi[i].reshape(1, W),
                               sp[i], reverse=(i == 1), **kw)

    zeros = jnp.zeros((B, 1, W), F32)
    hc_b, carry_b = run(x_ctx, zeros, 1)
    if gate_ctx is None:
        _, carry_f = run(x_ctx, zeros, 0)
        y_ctx = None
    else:
        y_ctx, carry_f = run(x_ctx, zeros, 0, other=hc_b, gate=gate_ctx)
    hl_b, _ = run(x_lat, carry_b, 1)
    y_lat, _ = run(x_lat, carry_f, 0, other=hl_b, gate=gate_lat)
    return y_lat, y_ctx


def _half_masks(shape):
    lane = lax.broadcasted_iota(jnp.int32, shape, len(shape) - 1)
    return lane < HEAD_DIM, lane >= HEAD_DIM


def _dot_t(a, b):
    return lax.dot_general(a, b, (((1,), (1,)), ((), ())), preferred_element_type=F32)


def _na_kernel(q_ref, k_ref, v_ref, kc_ref, vc_ref, bias_ref, o_ref, *, rows_per_step, n_rows):
    g = pl.program_id(1)
    nwin = NA_WIN_R * GRID_W
    zero = jnp.zeros((), BF16)

    n_pairs = NA_WIDTH // LANES
    lane_lo = lax.broadcasted_iota(jnp.int32, (GRID_W, LANES), 1) < HEAD_DIM

    def one_row(i, carry):
        r = g * rows_per_step + i
        rs = jnp.clip(r - NA_WIN_R // 2, 0, n_rows - NA_WIN_R)
        var = r - rs
        start = pl.multiple_of(rs * GRID_W, GRID_W)
        q0 = pl.multiple_of(i * GRID_W, GRID_W)
        cols = [slice(p * LANES, (p + 1) * LANES) for p in range(n_pairs)]
        scores = []
        for p in range(n_pairs):
            qp = q_ref[pl.ds(q0, GRID_W), cols[p]]
            lhs = jnp.concatenate([jnp.where(lane_lo, qp, zero), jnp.where(lane_lo, zero, qp)], axis=0)
            s_nb = _dot_t(lhs, k_ref[0, pl.ds(start, nwin), cols[p]]) + bias_ref[var, p]
            s_cx = _dot_t(lhs, kc_ref[0, :, cols[p]])
            scores.append((s_nb, s_cx))
        probs = []
        for s_nb, s_cx in scores:
            m = jnp.maximum(jnp.max(s_nb, axis=1, keepdims=True), jnp.max(s_cx, axis=1, keepdims=True))
            e_nb = jnp.exp(s_nb - m)
            e_cx = jnp.exp(s_cx - m)
            denom = jnp.sum(e_nb, axis=1, keepdims=True) + jnp.sum(e_cx, axis=1, keepdims=True)
            probs.append((e_nb.astype(BF16), e_cx.astype(BF16), denom))
        for p in range(n_pairs):
            e_nb, e_cx, denom = probs[p]
            o = (jnp.dot(e_nb, v_ref[0, pl.ds(start, nwin), cols[p]], preferred_element_type=F32)
                 + jnp.dot(e_cx, vc_ref[0, :, cols[p]], preferred_element_type=F32)) / denom
            o_ref[pl.ds(q0, GRID_W), cols[p]] = jnp.where(lane_lo, o[:GRID_W], o[GRID_W:]).astype(o_ref.dtype)
        return carry

    lax.fori_loop(0, rows_per_step, one_row, 0, unroll=2)


def na_bias_table(rpb):
    cols = jnp.arange(GRID_W)
    cs = jnp.clip(cols - NA_WIN_C // 2, 0, GRID_W - NA_WIN_C)
    valid = (cols[None, :] >= cs[:, None]) & (cols[None, :] < cs[:, None] + NA_WIN_C)
    cb_idx = jnp.clip(cols[None, :] - cols[:, None] + NA_WIN_C - 1, 0, 2 * NA_WIN_C - 2)
    var = jnp.arange(NA_WIN_R)
    rb_idx = jnp.arange(NA_WIN_R)[None, :] - var[:, None] + NA_WIN_R - 1
    t = rpb.astype(F32)[:, rb_idx]
    t = t[:, :, :, cb_idx]
    t = jnp.where(valid[None, None, None], t, NEG)
    t = t.transpose(1, 0, 3, 2, 4)
    return t.reshape(NA_WIN_R, NA_HEADS // 2, 2 * GRID_W, NA_WIN_R * GRID_W)


def neighbourhood_attention(q, k, v, k_ctx, v_ctx, bias):
    B, S, Wd = k.shape
    C = k_ctx.shape[1]
    n_rows = S // GRID_W
    rps = 8
    steps = n_rows // rps
    return pl.pallas_call(
        functools.partial(_na_kernel, rows_per_step=rps, n_rows=n_rows),
        grid=(B, steps),
        in_specs=[pl.BlockSpec((rps * GRID_W, Wd), lambda b, g: (b * steps + g, 0)),
                  pl.BlockSpec((1, S, Wd), lambda b, g: (b, 0, 0)),
                  pl.BlockSpec((1, S, Wd), lambda b, g: (b, 0, 0)),
                  pl.BlockSpec((1, C, Wd), lambda b, g: (b, 0, 0)),
                  pl.BlockSpec((1, C, Wd), lambda b, g: (b, 0, 0)),
                  pl.BlockSpec(bias.shape, lambda b, g: (0, 0, 0, 0))],
        out_specs=pl.BlockSpec((rps * GRID_W, Wd), lambda b, g: (b * steps + g, 0)),
        out_shape=jax.ShapeDtypeStruct((B * S, Wd), BF16),
        compiler_params=_params(("arbitrary", "arbitrary")),
        name="neighbourhood_attention",
    )(q, k, v, k_ctx, v_ctx, bias)


def _dup_halves(x):
    lo = x[:, :HEAD_DIM]
    hi = x[:, HEAD_DIM:]
    return jnp.concatenate([lo, lo], axis=1), jnp.concatenate([hi, hi], axis=1)


def _gqa_kernel(sink_ref, q_ref, kp_ref, kcur_ref, kn_ref, vp_ref, vcur_ref, vn_ref, kc_ref, vc_ref, o_ref,
                *, n_blk):
    bi = pl.program_id(1)
    zero = jnp.zeros((), BF16)
    k_loc = jnp.concatenate([kp_ref[0], kcur_ref[0], kn_ref[0]], axis=0)
    v_loc = jnp.concatenate([vp_ref[0], vcur_ref[0], vn_ref[0]], axis=0)
    k_loc_d = _dup_halves(k_loc)
    v_loc_d = _dup_halves(v_loc)
    kc_d = _dup_halves(kc_ref[0])
    vc_d = _dup_halves(vc_ref[0])
    span = 3 * GQA_BLOCK
    rows = GQA_GROUP * GQA_BLOCK
    qi = lax.broadcasted_iota(jnp.int32, (rows, span), 0) % GQA_BLOCK
    kj = lax.broadcasted_iota(jnp.int32, (rows, span), 1)
    valid = (jnp.abs(qi + GQA_WINDOW - kj) <= GQA_WINDOW)
    valid &= (kj >= GQA_BLOCK) | (bi > 0)
    valid &= (kj < 2 * GQA_BLOCK) | (bi < n_blk - 1)
    head_of_row = lax.broadcasted_iota(jnp.int32, (rows, 1), 0) // GQA_BLOCK
    lane_lo = lax.broadcasted_iota(jnp.int32, (GQA_BLOCK, LANES), 1) < HEAD_DIM

    def masked_q(h):
        qp = q_ref[:, (h // 2) * LANES:(h // 2 + 1) * LANES]
        return jnp.where(lane_lo, qp, zero) if h % 2 == 0 else jnp.where(lane_lo, zero, qp)

    scores = []
    for kv in range(GQA_KV_HEADS):
        heads = range(kv * GQA_GROUP, (kv + 1) * GQA_GROUP)
        lhs = jnp.concatenate([masked_q(h) for h in heads], axis=0)
        s_loc = jnp.where(valid, _dot_t(lhs, k_loc_d[kv]), NEG)
        s_cx = _dot_t(lhs, kc_d[kv])
        sink = jnp.zeros((rows, 1), F32)
        for n, h in enumerate(heads):
            sink = jnp.where(head_of_row == n, sink_ref[h], sink)
        scores.append((s_loc, s_cx, sink))
    probs = []
    for s_loc, s_cx, sink in scores:
        m = jnp.maximum(jnp.maximum(jnp.max(s_loc, axis=1, keepdims=True),
                                    jnp.max(s_cx, axis=1, keepdims=True)), sink)
        e_loc = jnp.exp(s_loc - m)
        e_cx = jnp.exp(s_cx - m)
        denom = (jnp.sum(e_loc, axis=1, keepdims=True) + jnp.sum(e_cx, axis=1, keepdims=True)
                 + jnp.exp(sink - m))
        probs.append((e_loc.astype(BF16), e_cx.astype(BF16), denom))
    outs = []
    for kv in range(GQA_KV_HEADS):
        e_loc, e_cx, denom = probs[kv]
        o = (jnp.dot(e_loc, v_loc_d[kv], preferred_element_type=F32)
             + jnp.dot(e_cx, vc_d[kv], preferred_element_type=F32)) / denom
        outs += [o[n * GQA_BLOCK:(n + 1) * GQA_BLOCK] for n in range(GQA_GROUP)]
    for p in range(GQA_WIDTH // LANES):
        o_ref[:, p * LANES:(p + 1) * LANES] = jnp.where(lane_lo, outs[2 * p], outs[2 * p + 1]).astype(o_ref.dtype)


def window_gqa(q, k, v, k_ctx, v_ctx, sink):
    B, S, KW = k.shape
    C = k_ctx.shape[1]
    n_blk = S // GQA_BLOCK
    blk = lambda f: pl.BlockSpec((1, GQA_BLOCK, KW), f)
    prev = lambda b, i, s: (b, jnp.maximum(i - 1, 0), 0)
    cur = lambda b, i, s: (b, i, 0)
    nxt = lambda b, i, s: (b, jnp.minimum(i + 1, n_blk - 1), 0)
    ctxs = pl.BlockSpec((1, C, KW), lambda b, i, s: (b, 0, 0))
    grid_spec = pltpu.PrefetchScalarGridSpec(
        num_scalar_prefetch=1,
        grid=(B, n_blk),
        in_specs=[pl.BlockSpec((GQA_BLOCK, GQA_WIDTH), lambda b, i, s: (b * n_blk + i, 0)),
                  blk(prev), blk(cur), blk(nxt), blk(prev), blk(cur), blk(nxt), ctxs, ctxs],
        out_specs=pl.BlockSpec((GQA_BLOCK, GQA_WIDTH), lambda b, i, s: (b * n_blk + i, 0)))
    return pl.pallas_call(
        functools.partial(_gqa_kernel, n_blk=n_blk),
        grid_spec=grid_spec,
        out_shape=jax.ShapeDtypeStruct((B * S, GQA_WIDTH), BF16),
        compiler_params=_params(("arbitrary", "arbitrary")),
        name="window_gqa",
    )(sink.astype(F32), q, k, k, k, v, v, v, k_ctx, v_ctx)


def _ctx_attn_kernel(sink_ref, qn_ref, kn_ref, vn_ref, qg_ref, kg_ref, vg_ref, on_ref, og_ref):
    zero = jnp.zeros((), BF16)
    kg_d = _dup_halves(kg_ref[0])
    vg_d = _dup_halves(vg_ref[0])
    for p in range(NA_WIDTH // LANES):
        cs = slice(p * LANES, (p + 1) * LANES)
        qn = qn_ref[0, :, cs]
        kn = kn_ref[0, :, cs]
        vn = vn_ref[0, :, cs]
        qg = qg_ref[0, :, cs]
        masks = _half_masks(qn.shape)
        acc_n = jnp.zeros(qn.shape, F32)
        acc_g = jnp.zeros(qg.shape, F32)
        for half in range(2):
            h = 2 * p + half
            s = _dot_t(jnp.where(masks[half], qn, zero), kn)
            m = jnp.max(s, axis=1, keepdims=True)
            e = jnp.exp(s - m)
            o = jnp.dot(e.astype(BF16), jnp.where(masks[half], vn, zero), preferred_element_type=F32)
            acc_n = acc_n + o / jnp.sum(e, axis=1, keepdims=True)
            kv = h // GQA_GROUP
            s = _dot_t(jnp.where(masks[half], qg, zero), kg_d[kv])
            sink = sink_ref[h]
            m = jnp.maximum(jnp.max(s, axis=1, keepdims=True), sink)
            e = jnp.exp(s - m)
            o = jnp.dot(e.astype(BF16), jnp.where(masks[half], vg_d[kv], zero), preferred_element_type=F32)
            acc_g = acc_g + o / (jnp.sum(e, axis=1, keepdims=True) + jnp.exp(sink - m))
        on_ref[0, :, cs] = acc_n.astype(on_ref.dtype)
        og_ref[0, :, cs] = acc_g.astype(og_ref.dtype)


def ctx_attention(qn, kn, vn, qg, kg, vg, sink):
    B, C, _ = qn.shape
    spec = lambda w: pl.BlockSpec((1, C, w), lambda b, s: (b, 0, 0))
    grid_spec = pltpu.PrefetchScalarGridSpec(
        num_scalar_prefetch=1, grid=(B,),
        in_specs=[spec(NA_WIDTH), spec(NA_WIDTH), spec(NA_WIDTH), spec(GQA_WIDTH), spec(GQA_KV_WIDTH),
                  spec(GQA_KV_WIDTH)],
        out_specs=[spec(NA_WIDTH), spec(GQA_WIDTH)])
    return pl.pallas_call(
        _ctx_attn_kernel, grid_spec=grid_spec,
        out_shape=[jax.ShapeDtypeStruct((B, C, NA_WIDTH), BF16), jax.ShapeDtypeStruct((B, C, GQA_WIDTH), BF16)],
        compiler_params=_params(("arbitrary",)),
        name="ctx_attention",
    )(sink.astype(F32), qn, kn, vn, qg, kg, vg)


def _layer_norm(y, g, b):
    mu = jnp.mean(y, axis=-1, keepdims=True)
    yc = y - mu
    var = jnp.mean(yc * yc, axis=-1, keepdims=True)
    return yc * lax.rsqrt(var + LN_EPS) * g + b


def _rows_to_tiles(ref, val):
    n = val.shape[0]
    for s in range(SUBLANES):
        ref[pl.ds(s, n, stride=SUBLANES), :] = val[:, s * LANES:(s + 1) * LANES]


def _rows_from_tiles(ref, n):
    return jnp.concatenate([ref[pl.ds(s, n, stride=SUBLANES), :] for s in range(SUBLANES)], axis=1)


def _outproj_kernel(yrg_ref, ona_ref, oga_ref, x_ref, w_ref, g1_ref, lg_ref, lb_ref, sc_ref, sh_ref,
                    rwh_ref, rwl_ref, rb_ref, x1_ref, h2_ref, route_ref, *, alpha):
    o0, o1, o2 = RG_WIDTH, RG_WIDTH + NA_WIDTH, RG_WIDTH + NA_WIDTH + GQA_WIDTH
    mix = (jnp.dot(yrg_ref[...], w_ref[0:o0, :], preferred_element_type=F32)
           + jnp.dot(ona_ref[...], w_ref[o0:o1, :], preferred_element_type=F32)
           + jnp.dot(oga_ref[...], w_ref[o1:o2, :], preferred_element_type=F32))
    x1 = _layer_norm(alpha * x_ref[...] + g1_ref[0] * mix, lg_ref[...], lb_ref[...])
    x1_ref[...] = x1
    h2 = x1 * sc_ref[0] + sh_ref[0]
    _rows_to_tiles(h2_ref, h2)
    hi = h2.astype(BF16)
    lo = (h2 - hi.astype(F32)).astype(BF16)
    logits = (jnp.dot(hi, rwh_ref[...], preferred_element_type=F32)
              + jnp.dot(hi, rwl_ref[...], preferred_element_type=F32)
              + jnp.dot(lo, rwh_ref[...], preferred_element_type=F32)) + rb_ref[...]

    lane = lax.broadcasted_iota(jnp.int32, logits.shape, 1)
    big = jnp.int32(1 << 20)
    gl = jnp.where(lane < N_GROUPS, logits, NEG)
    gmax = jnp.max(gl, axis=1, keepdims=True)
    gsum = jnp.sum(jnp.exp(gl - gmax), axis=1, keepdims=True)
    p_top = 1.0 / gsum
    g_idx = jnp.min(jnp.where(gl == gmax, lane, big), axis=1, keepdims=True)
    lo_lane = N_GROUPS + g_idx * EXPERTS_PER_GROUP
    el = jnp.where((lane >= lo_lane) & (lane < lo_lane + EXPERTS_PER_GROUP), logits, NEG)
    m1 = jnp.max(el, axis=1, keepdims=True)
    i1 = jnp.min(jnp.where(el == m1, lane, big), axis=1, keepdims=True)
    el2 = jnp.where(lane == i1, NEG, el)
    m2 = jnp.max(el2, axis=1, keepdims=True)
    i2 = jnp.min(jnp.where(el2 == m2, lane, big), axis=1, keepdims=True)
    e2 = jnp.exp(m2 - m1)
    w1 = p_top / (1.0 + e2)
    w2 = p_top * e2 / (1.0 + e2)
    route = jnp.where(lane == 0, (i1 - N_GROUPS).astype(F32),
                      jnp.where(lane == 1, (i2 - N_GROUPS).astype(F32),
                                jnp.where(lane == 2, w1, jnp.where(lane == 3, w2, 0.0))))
    route_ref[...] = route


def _outproj_kernel_two_sets(yrg_ref, ona_ref, oga_ref, x_ref, yrgc_ref, onac_ref, ogac_ref, xc_ref, *refs,
                             alpha, n_lat_tiles):
    shared_in = refs[:9]
    x1_ref, x1c_ref, h2_ref, route_ref = refs[9:]
    i = pl.program_id(0)

    @pl.when(i < n_lat_tiles)
    def _():
        _outproj_kernel(yrg_ref, ona_ref, oga_ref, x_ref, *shared_in, x1_ref, h2_ref, route_ref, alpha=alpha)

    @pl.when(i >= n_lat_tiles)
    def _():
        _outproj_kernel(yrgc_ref, onac_ref, ogac_ref, xc_ref, *shared_in, x1c_ref, h2_ref, route_ref, alpha=alpha)


def out_projection(lat, ctx, w_out, g1, ln_g, ln_b, sc2, sh2, rw_hi, rw_lo, rb, *, seq, alpha):
    N, D = lat[3].shape
    assert D == ROW_TILE
    tm = min(512, seq)
    per = seq // tm
    n_lat_tiles = N // tm
    n_batch = N // seq
    n_ctx = 0 if ctx is None else ctx[3].shape[0]
    assert n_ctx % tm == 0
    n_tiles = n_lat_tiles + n_ctx // tm
    lat_row = lambda w: pl.BlockSpec((tm, w), lambda i: (jnp.minimum(i, n_lat_tiles - 1), 0))
    ctx_row = lambda w: pl.BlockSpec((tm, w), lambda i: (jnp.maximum(i - n_lat_tiles, 0), 0))
    all_row = lambda r, w: pl.BlockSpec((r, w), lambda i: (i, 0))
    per_b = pl.BlockSpec((1, 1, D), lambda i: (jnp.minimum(i // per, n_batch), 0, 0))
    vec = pl.BlockSpec((1, D), lambda i: (0, 0))
    full = lambda a: pl.BlockSpec(a.shape, lambda i: (0,) * a.ndim)
    widths = (RG_WIDTH, NA_WIDTH, GQA_WIDTH, D)
    shared_specs = [full(w_out), per_b, vec, vec, per_b, per_b, full(rw_hi), full(rw_lo), full(rb)]
    shared_args = [w_out, g1, ln_g, ln_b, sc2, sh2, rw_hi, rw_lo, rb]
    shared_out_specs = [all_row(tm * SUBLANES, LANES), all_row(tm, ROUTE_LANES)]
    shared_out_shape = [jax.ShapeDtypeStruct((n_tiles * tm * SUBLANES, LANES), F32),
                        jax.ShapeDtypeStruct((n_tiles * tm, ROUTE_LANES), F32)]
    if ctx is None:
        x1, h2, route = pl.pallas_call(
            functools.partial(_outproj_kernel, alpha=alpha),
            grid=(n_tiles,),
            in_specs=[lat_row(w) for w in widths] + shared_specs,
            out_specs=[lat_row(D)] + shared_out_specs,
            out_shape=[jax.ShapeDtypeStruct((N, D), F32)] + shared_out_shape,
            compiler_params=_params(("arbitrary",)),
            name="out_projection",
        )(*lat, *shared_args)
        return x1, None, h2, route
    return pl.pallas_call(
        functools.partial(_outproj_kernel_two_sets, alpha=alpha, n_lat_tiles=n_lat_tiles),
        grid=(n_tiles,),
        in_specs=[lat_row(w) for w in widths] + [ctx_row(w) for w in widths] + shared_specs,
        out_specs=[lat_row(D), ctx_row(D)] + shared_out_specs,
        out_shape=[jax.ShapeDtypeStruct((N, D), F32), jax.ShapeDtypeStruct((n_ctx, D), F32)] + shared_out_shape,
        compiler_params=_params(("arbitrary",)),
        name="out_projection",
    )(*lat, *ctx, *shared_args)


def _expert_kernel(be_ref, nu_ref, x_ref, wg_ref, wu_ref, wd_ref, y_ref, wg_s, wu_s, wd_s):
    i = pl.program_id(0)

    @pl.when((i == 0) | (be_ref[i] != be_ref[jnp.maximum(i - 1, 0)]))
    def _():
        wg_s[...] = wg_ref[0].astype(BF16)
        wu_s[...] = wu_ref[0].astype(BF16)
        wd_s[...] = wd_ref[0].astype(BF16)

    @pl.when(i < nu_ref[0])
    def _():
        x = _rows_from_tiles(x_ref, MOE_BM).astype(BF16)
        a = jnp.dot(x, wg_s[...], preferred_element_type=F32)
        u = jnp.dot(x, wu_s[...], preferred_element_type=F32)
        hmid = (a * jax.nn.sigmoid(a) * u).astype(BF16)
        _rows_to_tiles(y_ref, jnp.dot(hmid, wd_s[...], preferred_element_type=F32))

    @pl.when(i >= nu_ref[0])
    def _():
        y_ref[...] = jnp.zeros(y_ref.shape, y_ref.dtype)


def expert_mlp(xs, blk_e, n_used, w_gate, w_up, w_down):
    cap = xs.shape[0] // SUBLANES
    D, FF = w_gate.shape[1], w_gate.shape[2]
    n_blk = cap // MOE_BM
    tile_blk = (MOE_BM * SUBLANES, LANES)
    grid_spec = pltpu.PrefetchScalarGridSpec(
        num_scalar_prefetch=2, grid=(n_blk,),
        in_specs=[pl.BlockSpec(tile_blk, lambda i, be, nu: (jnp.minimum(i, nu[0] - 1), 0)),
                  pl.BlockSpec((1, D, FF), lambda i, be, nu: (be[i], 0, 0)),
                  pl.BlockSpec((1, D, FF), lambda i, be, nu: (be[i], 0, 0)),
                  pl.BlockSpec((1, FF, D), lambda i, be, nu: (be[i], 0, 0))],
        out_specs=pl.BlockSpec(tile_blk, lambda i, be, nu: (i, 0)),
        scratch_shapes=[pltpu.VMEM((D, FF), BF16), pltpu.VMEM((D, FF), BF16), pltpu.VMEM((FF, D), BF16)])
    return pl.pallas_call(
        _expert_kernel, grid_spec=grid_spec,
        out_shape=jax.ShapeDtypeStruct((cap * SUBLANES, LANES), F32),
        compiler_params=_params(("arbitrary",)),
        name="expert_mlp",
    )(blk_e, n_used, xs, w_gate, w_up, w_down)


def _tile_rows(row):
    return pl.ds(pl.multiple_of(row * SUBLANES, SUBLANES), SUBLANES)


def _gather_copy(y_hbm, ybuf, sem, src_slot, t, k, buf):
    return pltpu.make_async_copy(y_hbm.at[_tile_rows(src_slot)], ybuf.at[buf, k, _tile_rows(t)], sem.at[buf])


def _combine_kernel(dest_ref, destn_ref, x_ref, route_ref, g2_ref, lg_ref, lb_ref, y_hbm, o_ref,
                    ybuf, sem, *, alpha, tm):
    i = pl.program_id(0)
    buf = i % 2

    def issue(dest, b):
        def body(t, c):
            for k in range(2):
                _gather_copy(y_hbm, ybuf, sem, dest[2 * t + k], t, k, b).start(priority=k)
            return c
        lax.fori_loop(0, tm, body, 0, unroll=8)

    @pl.when(i == 0)
    def _():
        issue(dest_ref, 0)

    @pl.when(i + 1 < pl.num_programs(0))
    def _():
        issue(destn_ref, 1 - buf)

    def drain(t, c):
        for k in range(2):
            _gather_copy(y_hbm, ybuf, sem, 0, t, k, buf).wait()
        return c
    lax.fori_loop(0, tm, drain, 0, unroll=8)

    route = route_ref[...]
    f = (route[:, 2:3] * _rows_from_tiles(ybuf.at[buf, 0], tm)
         + route[:, 3:4] * _rows_from_tiles(ybuf.at[buf, 1], tm))
    o_ref[...] = _layer_norm(alpha * x_ref[...] + g2_ref[0] * f, lg_ref[...], lb_ref[...])


def combine_norm(x1, y, dest, route, g2, ln_g, ln_b, *, seq, alpha, row_off=0):
    N, D = x1.shape
    tm = min(COMBINE_TM, seq)
    per = seq // tm
    off = row_off // tm
    n_steps = N // tm
    row = lambda w: pl.BlockSpec((tm, w), lambda i: (i, 0))
    vec = pl.BlockSpec((1, D), lambda i: (0, 0))
    dest_spec = lambda f: pl.BlockSpec((tm * 2,), f, memory_space=pltpu.SMEM)
    return pl.pallas_call(
        functools.partial(_combine_kernel, alpha=alpha, tm=tm),
        grid=(n_steps,),
        in_specs=[dest_spec(lambda i: (i + off,)),
                  dest_spec(lambda i: (jnp.minimum(i + 1, n_steps - 1) + off,)),
                  row(D), pl.BlockSpec((tm, ROUTE_LANES), lambda i: (i + off, 0)),
                  pl.BlockSpec((1, 1, D), lambda i: (i // per, 0, 0)), vec, vec,
                  pl.BlockSpec(memory_space=pl.ANY)],
        out_specs=row(D),
        out_shape=jax.ShapeDtypeStruct((N, D), F32),
        scratch_shapes=[pltpu.VMEM((2, 2, tm * SUBLANES, LANES), F32), pltpu.SemaphoreType.DMA((2,))],
        compiler_params=_params(("arbitrary",)),
        name="combine_norm",
    )(dest, dest, x1, route, g2, ln_g, ln_b, y)


def _rank_kernel(route_ref, meta_ref, cnt_ref, run, *, tm):
    i = pl.program_id(0)

    @pl.when(i == 0)
    def _():
        run[...] = jnp.zeros(run.shape, F32)

    route = route_ref[...]
    lane = lax.broadcasted_iota(jnp.int32, route.shape, 1)
    e0 = route[:, 0:1].astype(jnp.int32)
    e1 = route[:, 1:2].astype(jnp.int32)
    oh0 = (lane == e0).astype(F32)
    oh1 = (lane == e1).astype(F32)
    both = oh0 + oh1
    earlier = (lax.broadcasted_iota(jnp.int32, (tm, tm), 1) < lax.broadcasted_iota(jnp.int32, (tm, tm), 0))
    before = jnp.dot(earlier.astype(BF16), both.astype(BF16), preferred_element_type=F32) + run[...]
    r0 = jnp.sum(before * oh0, axis=1, keepdims=True).astype(jnp.int32)
    r1 = jnp.sum(before * oh1, axis=1, keepdims=True).astype(jnp.int32)
    meta_ref[...] = jnp.where(lane == 0, e0, jnp.where(lane == 1, e1, jnp.where(lane == 2, r0,
                                                                               jnp.where(lane == 3, r1, 0))))
    run[...] = run[...] + jnp.sum(both, axis=0, keepdims=True)
    cnt_ref[...] = run[...]


def route_rank(route):
    Nt = route.shape[0]
    tm = 512
    return pl.pallas_call(
        functools.partial(_rank_kernel, tm=tm),
        grid=(Nt // tm,),
        in_specs=[pl.BlockSpec((tm, ROUTE_LANES), lambda i: (i, 0))],
        out_specs=[pl.BlockSpec((tm, ROUTE_LANES), lambda i: (i, 0)),
                   pl.BlockSpec((1, ROUTE_LANES), lambda i: (0, 0))],
        out_shape=[jax.ShapeDtypeStruct((Nt, ROUTE_LANES), jnp.int32),
                   jax.ShapeDtypeStruct((1, ROUTE_LANES), F32)],
        scratch_shapes=[pltpu.VMEM((1, ROUTE_LANES), F32)],
        compiler_params=_params(("arbitrary",)),
        name="route_rank",
    )(route)


def _dispatch_kernel(sp_ref, cnt_ref, dest_ref, h_ref, xs_hbm, zrow, sem, zsem, *, tm, cap):
    i = pl.program_id(0)

    @pl.when(i == 0)
    def _():
        zrow[...] = jnp.zeros(zrow.shape, F32)

        def zero_copy(row):
            return pltpu.make_async_copy(zrow, xs_hbm.at[_tile_rows(row)], zsem)

        def per_expert(e, total):
            cnt = cnt_ref[e]
            first = sp_ref[e] + cnt
            n_pad = (MOE_BM - cnt % MOE_BM) % MOE_BM

            def per_row(j, c):
                zero_copy(first + j).start()
                return c
            lax.fori_loop(0, n_pad, per_row, 0)
            return total + n_pad
        total = lax.fori_loop(0, N_EXPERTS, per_expert, 0)

        last = N_EXPERTS - 1
        used = sp_ref[last] + (cnt_ref[last] + MOE_BM - 1) // MOE_BM * MOE_BM

        def per_tail_row(j, c):
            zero_copy(used + j).start()
            return c
        lax.fori_loop(0, cap - used, per_tail_row, 0)

        def wait_row(j, c):
            zero_copy(0).wait()
            return c
        lax.fori_loop(0, total + cap - used, wait_row, 0)

    def row_copy(t, slot):
        return pltpu.make_async_copy(h_ref.at[_tile_rows(t)], xs_hbm.at[_tile_rows(slot)], sem)

    def issue(t, c):
        for k in range(2):
            row_copy(t, dest_ref[2 * t + k]).start(priority=k)
        return c
    lax.fori_loop(0, tm, issue, 0, unroll=8)

    def drain(t, c):
        for k in range(2):
            row_copy(t, 0).wait()
        return c
    lax.fori_loop(0, tm, drain, 0, unroll=8)


def dispatch_rows(h2_tiles, dest, start_pad, counts, cap):
    Nt = h2_tiles.shape[0] // SUBLANES
    tm = 512
    grid_spec = pltpu.PrefetchScalarGridSpec(
        num_scalar_prefetch=2, grid=(Nt // tm,),
        in_specs=[pl.BlockSpec((tm * 2,), lambda i, sp, cn: (i,), memory_space=pltpu.SMEM),
                  pl.BlockSpec((tm * SUBLANES, LANES), lambda i, sp, cn: (i, 0))],
        out_specs=pl.BlockSpec(memory_space=pl.ANY),
        scratch_shapes=[pltpu.VMEM((SUBLANES, LANES), F32), pltpu.SemaphoreType.DMA(()),
                        pltpu.SemaphoreType.DMA(())])
    return pl.pallas_call(
        functools.partial(_dispatch_kernel, tm=tm, cap=cap),
        grid_spec=grid_spec,
        out_shape=jax.ShapeDtypeStruct((cap * SUBLANES, LANES), F32),
        compiler_params=_params(("arbitrary",)),
        name="dispatch_rows",
    )(start_pad, counts, dest, h2_tiles)


def moe(h2_tiles, route_all, w_gate, w_up, w_down):
    Nt = route_all.shape[0]
    meta_wide, cnt = route_rank(route_all)
    counts = cnt[0, :N_EXPERTS].astype(jnp.int32)
    padded = (counts + MOE_BM - 1) // MOE_BM * MOE_BM
    end_pad = jnp.cumsum(padded)
    start_pad = end_pad - padded
    eid, rank = meta_wide[:, 0:2], meta_wide[:, 2:4]
    first = jnp.sum(jnp.where(eid[:, :, None] == jnp.arange(N_EXPERTS, dtype=jnp.int32), start_pad, 0), axis=-1)
    dest = (first + rank).reshape(-1)
    cap = -(-(2 * Nt) // MOE_BM) * MOE_BM + N_EXPERTS * MOE_BM
    blk_start = jnp.arange(cap // MOE_BM, dtype=jnp.int32) * MOE_BM
    blk_e = jnp.minimum(jnp.sum((end_pad[None, :] <= blk_start[:, None]).astype(jnp.int32), axis=1), N_EXPERTS - 1)
    n_used = (end_pad[-1] // MOE_BM).astype(jnp.int32).reshape(1)
    xs = dispatch_rows(h2_tiles, dest, start_pad, counts, cap)
    y = expert_mlp(xs, blk_e, n_used, w_gate, w_up, w_down)
    return y, dest


def _split6(m):
    return jnp.split(m, 6, axis=-1)


def kernel(x, c, ctx, c_ctx, w_ada, b_ada, w_in, conv_w, conv_b, rg_a_w, rg_a_b, rg_i_w, rg_i_b, rg_lambda,
           na_rpb, gqa_sink, w_out, ln1_g, ln1_b, router_g_w, router_g_b, router_e_w, router_e_b,
           exp_w_gate, exp_w_up, exp_w_down, ln2_g, ln2_b):
    B, S, D = x.shape
    C = ctx.shape[1]
    depth = w_ada.shape[0]
    alpha = (2 * depth) ** 0.25

    n_cond = -(-(B + 1) // 8) * 8
    cond = jnp.zeros((n_cond, D), F32).at[:B].set(c).at[B].set(c_ctx)
    mod_all = ada_modulation(cond, w_ada, b_ada)
    cos_t, sin_t = rope_tables(S)
    cos_c = jnp.zeros((C, LANES), F32)

    x_lat = x.reshape(B * S, D)
    x_ctx = ctx.reshape(B * C, D)
    for l in range(depth):
        need_ctx = l < depth - 1
        sh1, sc1, g1, sh2, sc2, g2 = [m[:, None, :] for m in _split6(mod_all[l, :B])]
        csh1, csc1, cg1, csh2, csc2, cg2 = [jnp.broadcast_to(m[None, None, :], (B, 1, D))
                                            for m in _split6(mod_all[l, B])]
        w_in_b = w_in[l].astype(BF16)
        w_out_b = w_out[l].astype(BF16)
        rgx, rgg, nak, nav, naq, gk, gv, gq = in_projection(x_lat, 1.0 + sc1, sh1, w_in_b, cos_t, sin_t,
                                                            seq=S, rope=True)
        crgx, crgg, cnak, cnav, cnaq, cgk, cgv, cgq = in_projection(x_ctx, 1.0 + csc1, csh1, w_in_b, cos_c, cos_c,
                                                                    seq=C, rope=False)
        r3 = lambda a, t: a.reshape(B, t, a.shape[-1])
        y_rg, y_rg_c = rglru_mixer(r3(rgx, S), r3(crgx, C), r3(rgg, S), r3(crgg, C) if need_ctx else None,
                                   conv_w[l], conv_b[l], rg_a_w[l], rg_a_b[l], rg_i_w[l], rg_i_b[l], rg_lambda[l])
        o_na = neighbourhood_attention(naq, r3(nak, S), r3(nav, S), r3(cnak, C), r3(cnav, C),
                                       na_bias_table(na_rpb[l]))
        o_ga = window_gqa(gq, r3(gk, S), r3(gv, S), r3(cgk, C), r3(cgv, C), gqa_sink[l])

        rw = jnp.zeros((D, ROUTE_LANES), F32).at[:, :N_GROUPS].set(router_g_w[l])
        rw = rw.at[:, N_GROUPS:N_GROUPS + N_EXPERTS].set(router_e_w[l])
        rw_hi = rw.astype(BF16)
        rw_lo = (rw - rw_hi.astype(F32)).astype(BF16)
        rb = jnp.zeros((1, ROUTE_LANES), F32).at[0, :N_GROUPS].set(router_g_b[l])
        rb = rb.at[0, N_GROUPS:N_GROUPS + N_EXPERTS].set(router_e_b[l])
        lg1, lb1 = ln1_g[l].reshape(1, D), ln1_b[l].reshape(1, D)
        lg2, lb2 = ln2_g[l].reshape(1, D), ln2_b[l].reshape(1, D)

        n_lat = B * S
        lat_set = (y_rg.reshape(n_lat, RG_WIDTH), o_na, o_ga, x_lat)
        ctx_set = None
        if need_ctx:
            o_na_c, o_ga_c = ctx_attention(r3(cnaq, C), r3(cnak, C), r3(cnav, C), r3(cgq, C), r3(cgk, C),
                                           r3(cgv, C), gqa_sink[l])
            ctx_set = (y_rg_c.reshape(B * C, RG_WIDTH), o_na_c.reshape(B * C, NA_WIDTH),
                       o_ga_c.reshape(B * C, GQA_WIDTH), x_ctx)
        with_ctx_row = lambda m, mc: jnp.concatenate([m, mc[:1]], axis=0)
        x1, x1c, h2, route = out_projection(lat_set, ctx_set, w_out_b, with_ctx_row(g1, cg1), lg1, lb1,
                                            with_ctx_row(1.0 + sc2, 1.0 + csc2), with_ctx_row(sh2, csh2),
                                            rw_hi, rw_lo, rb, seq=S, alpha=alpha)
        y, dest = moe(h2, route, exp_w_gate[l], exp_w_up[l], exp_w_down[l])
        if need_ctx:
            x_ctx = combine_norm(x1c, y, dest, route, cg2, lg2, lb2, seq=C, alpha=alpha, row_off=n_lat)
        x_lat = combine_norm(x1, y, dest, route, g2, lg2, lb2, seq=S, alpha=alpha)
    return x_lat.reshape(B, S, D)
```
